```python
import jax, jax.numpy as jnp
from jax import lax
import numpy as np

D_MODEL = 2048
BATCH = 4
SEQ = 2048
DEPTH = 2

N_MIXERS = 2
N_LAYERS_A = (DEPTH + N_MIXERS - 1) // N_MIXERS
N_LAYERS_B = DEPTH // N_MIXERS
CHUNK = 64
EPS = 1e-6

A_HEADS = 8
A_DV = D_MODEL // A_HEADS
A_DK = A_DV // 2
A_QK = A_HEADS * A_DK
A_V = A_HEADS * A_DV
A_IN = 2 * A_QK + 2 * A_V + 4 * A_HEADS

B_EXPAND = 128
B_HEADS = D_MODEL // B_EXPAND
B_DK = B_EXPAND
B_DV = D_MODEL // B_HEADS
B_F = B_HEADS * B_DK
B_V = B_HEADS * B_DV
B_IN = 3 * B_F + 2 * B_V

D_FF = 5504
CONV_W = 3

kernel_name = "hybrid_mlstm_hgrn2_convglu_encoder"


def rmsnorm(x, g):
    xf = x.astype(jnp.float32)
    xf = xf * lax.rsqrt(jnp.mean(xf * xf, axis=-1, keepdims=True) + EPS)
    return xf.astype(x.dtype) * g


def head_rmsnorm(y, g, n_heads):
    b, s, w = y.shape
    yf = y.astype(jnp.float32).reshape(b, s, n_heads, w // n_heads)
    yf = yf * lax.rsqrt(jnp.mean(yf * yf, axis=-1, keepdims=True) + EPS)
    return yf.reshape(b, s, w).astype(y.dtype) * g


def to_chunks(t):
    b, s = t.shape[0], t.shape[1]
    t = t.reshape((b, s // CHUNK, CHUNK) + t.shape[2:])
    perm = (1, 0, 3, 2) + tuple(range(4, t.ndim))
    return t.transpose(perm)


def from_chunks(t):
    nc, b, h, l, d = t.shape
    return t.transpose(1, 0, 3, 2, 4).reshape(b, nc * l, h, d)


def _mlstm_chunk(carry, xs):
    c_st, n_st, m_st = carry
    q, k, v, ig, lf = xs
    L = q.shape[2]
    b = jnp.cumsum(lf, axis=-1)
    g_tot = b[..., -1]
    lower = jnp.tril(jnp.ones((L, L), dtype=bool))
    d = jnp.where(lower, b[..., :, None] - b[..., None, :] + ig[..., None, :], -jnp.inf)
    m_inter = b + m_st[..., None]
    m_t = jnp.maximum(jnp.max(d, axis=-1), m_inter)
    s = jnp.einsum('bhtk,bhsk->bhts', q, k) * jnp.exp(d - m_t[..., None])
    w_inter = jnp.exp(m_inter - m_t)
    num = jnp.einsum('bhts,bhsv->bhtv', s, v) + w_inter[..., None] * jnp.einsum('bhtk,bhvk->bhtv', q, c_st)
    den = jnp.sum(s, axis=-1) + w_inter * jnp.einsum('bhtk,bhk->bht', q, n_st)
    h = num / jnp.maximum(jnp.abs(den), jnp.exp(-m_t))[..., None]
    a = g_tot[..., None] - b + ig
    m_new = jnp.maximum(g_tot + m_st, jnp.max(a, axis=-1))
    w_s = jnp.exp(a - m_new[..., None])
    decay = jnp.exp(g_tot + m_st - m_new)
    c_st = decay[..., None, None] * c_st + jnp.einsum('bhs,bhsv,bhsk->bhvk', w_s, v, k)
    n_st = decay[..., None] * n_st + jnp.einsum('bhs,bhsk->bhk', w_s, k)
    return (c_st, n_st, m_new), h


def mlstm_direction(q, k, v, ig, lf):
    b, _, h, dk = q.shape
    dv = v.shape[-1]
    init = (jnp.zeros((b, h, dv, dk), jnp.float32), jnp.zeros((b, h, dk), jnp.float32),
            jnp.zeros((b, h), jnp.float32))
    xs = (to_chunks(q), to_chunks(k), to_chunks(v), to_chunks(ig), to_chunks(lf))
    _, out = lax.scan(_mlstm_chunk, init, xs)
    return from_chunks(out)


def mlstm_mixer(hn, w_in, b_gate, head_g, w_out):
    b, s, _ = hn.shape
    p = hn @ w_in
    q, k, v, o, gates = jnp.split(p, [A_QK, 2 * A_QK, 2 * A_QK + A_V, 2 * A_QK + 2 * A_V], axis=-1)
    gates = (gates + b_gate).astype(jnp.float32).reshape(b, s, 4, A_HEADS)
    ig_f, lf_f = gates[:, :, 0], jax.nn.log_sigmoid(gates[:, :, 1])
    ig_b, lf_b = gates[:, :, 2], jax.nn.log_sigmoid(gates[:, :, 3])
    q = q.astype(jnp.float32).reshape(b, s, A_HEADS, A_DK) * (A_DK ** -0.5)
    k = k.astype(jnp.float32).reshape(b, s, A_HEADS, A_DK)
    v = v.astype(jnp.float32).reshape(b, s, A_HEADS, A_DV)
    fl = lambda t: jnp.flip(t, axis=1)
    y = mlstm_direction(q, k, v, ig_f, lf_f) + fl(mlstm_direction(fl(q), fl(k), fl(v), fl(ig_b), fl(lf_b)))
    y = y.reshape(b, s, A_V).astype(hn.dtype)
    y = head_rmsnorm(y, head_g, A_HEADS) * jax.nn.sigmoid(o)
    return y @ w_out


def _hgrn_chunk(s_st, xs):
    q, k, lf, v = xs
    L = q.shape[2]
    b = jnp.cumsum(lf, axis=2)
    lower = jnp.tril(jnp.ones((L, L), dtype=bool))
    rel = jnp.where(lower[:, :, None], b[:, :, :, None, :] - b[:, :, None, :, :], -jnp.inf)
    attn = jnp.einsum('bhtsk,bhsk->bhts', jnp.exp(rel) * q[:, :, :, None, :], k)
    o = jnp.einsum('bhts,bhsv->bhtv', attn, v) + jnp.einsum('bhtk,bhkv->bhtv', q * jnp.exp(b), s_st)
    b_end = b[:, :, -1]
    s_st = jnp.exp(b_end)[..., None] * s_st + jnp.einsum('bhsk,bhsv->bhkv', k * jnp.exp(b_end[:, :, None] - b), v)
    return s_st, o


def hgrn_direction(q, k, lf, v):
    b, _, h, dk = q.shape
    dv = v.shape[-1]
    init = jnp.zeros((b, h, dk, dv), jnp.float32)
    _, out = lax.scan(_hgrn_chunk, init, (to_chunks(q), to_chunks(k), to_chunks(lf), to_chunks(v)))
    return from_chunks(out)


def hgrn_mixer(hn, w_in, lb, head_g, w_out):
    b, s, _ = hn.shape
    p = hn @ w_in
    q, i, g, f_f, f_b = jnp.split(p, [B_F, B_F + B_V, B_F + 2 * B_V, 2 * B_F + 2 * B_V], axis=-1)
    log_lb, log_1mlb = jnp.log(lb), jnp.log1p(-lb)

    def gate(f):
        f = f.astype(jnp.float32)
        log_f = jnp.logaddexp(log_lb, log_1mlb + jax.nn.log_sigmoid(f))
        k = (1.0 - lb) * jax.nn.sigmoid(-f)
        return (k.reshape(b, s, B_HEADS, B_DK), log_f.reshape(b, s, B_HEADS, B_DK))

    k_f, lf_f = gate(f_f)
    k_b, lf_b = gate(f_b)
    q = q.astype(jnp.float32).reshape(b, s, B_HEADS, B_DK)
    v = i.astype(jnp.float32).reshape(b, s, B_HEADS, B_DV)
    fl = lambda t: jnp.flip(t, axis=1)
    y = hgrn_direction(q, k_f, lf_f, v) + fl(hgrn_direction(fl(q), fl(k_b), fl(lf_b), fl(v)))
    y = y.reshape(b, s, B_V).astype(hn.dtype)
    y = head_rmsnorm(y, head_g, B_HEADS) * jax.nn.silu(g)
    return y @ w_out


def conv_glu(hn, w_up, conv_w, conv_b, w_down):
    u = hn @ w_up
    a, v = jnp.split(u, 2, axis=-1)
    s = a.shape[1]
    pad = CONV_W // 2
    ap = jnp.pad(a, ((0, 0), (pad, pad), (0, 0)))
    c = conv_b
    for j in range(CONV_W):
        c = c + ap[:, j:j + s] * conv_w[j]
    return (jax.nn.gelu(c, approximate=False) * v) @ w_down


def setup_inputs(seed: int = 0) -> dict:
    key = jax.random.key(seed)
    ks = jax.random.split(key, 20)
    f32 = jnp.float32
    nrm = lambda k, shape, scale: jax.random.normal(k, shape, f32) * scale
    i_bias = -3.0 + 0.1 * jax.random.normal(ks[0], (N_LAYERS_A, 2, A_HEADS), f32)
    f_bias = jnp.linspace(3.0, 6.0, A_HEADS, dtype=f32) + 0.1 * jax.random.normal(ks[1], (N_LAYERS_A, 2, A_HEADS), f32)
    b_gate = jnp.stack([i_bias[:, 0], f_bias[:, 0], i_bias[:, 1], f_bias[:, 1]], axis=1).reshape(N_LAYERS_A, 4 * A_HEADS)
    return {
        "x": jax.random.normal(ks[2], (BATCH, SEQ, D_MODEL), f32),
        "norm_mix_g": 1.0 + nrm(ks[3], (DEPTH, D_MODEL), 0.02),
        "norm_ffn_g": 1.0 + nrm(ks[4], (DEPTH, D_MODEL), 0.02),
        "mlstm_w_in": nrm(ks[5], (N_LAYERS_A, D_MODEL, A_IN), D_MODEL ** -0.5),
        "mlstm_b_gate": b_gate,
        "mlstm_head_g": 1.0 + nrm(ks[6], (N_LAYERS_A, A_V), 0.02),
        "mlstm_w_out": nrm(ks[7], (N_LAYERS_A, A_V, D_MODEL), A_V ** -0.5),
        "hgrn_w_in": nrm(ks[8], (N_LAYERS_B, D_MODEL, B_IN), D_MODEL ** -0.5),
        "hgrn_lb": nrm(ks[9], (DEPTH, B_F), 0.5),
        "hgrn_head_g": 1.0 + nrm(ks[10], (N_LAYERS_B, B_V), 0.02),
        "hgrn_w_out": nrm(ks[11], (N_LAYERS_B, B_V, D_MODEL), B_V ** -0.5),
        "ffn_w_up": nrm(ks[12], (DEPTH, D_MODEL, 2 * D_FF), D_MODEL ** -0.5),
        "ffn_conv_w": nrm(ks[13], (DEPTH, CONV_W, D_FF), CONV_W ** -0.5),
        "ffn_conv_b": nrm(ks[14], (DEPTH, D_FF), 0.02),
        "ffn_w_down": nrm(ks[15], (DEPTH, D_FF, D_MODEL), D_FF ** -0.5),
        "final_g": 1.0 + nrm(ks[16], (D_MODEL,), 0.02),
    }


def reference(x, norm_mix_g, norm_ffn_g, mlstm_w_in, mlstm_b_gate, mlstm_head_g, mlstm_w_out,
              hgrn_w_in, hgrn_lb, hgrn_head_g, hgrn_w_out, ffn_w_up, ffn_conv_w, ffn_conv_b,
              ffn_w_down, final_g):
    sm = jax.nn.softmax(hgrn_lb.astype(jnp.float32), axis=0)
    lower_bounds = jnp.cumsum(sm, axis=0) - sm[0]
    h = x
    for layer in range(DEPTH):
        j = layer // N_MIXERS
        hn = rmsnorm(h, norm_mix_g[layer])
        if layer % N_MIXERS == 0:
            mix = mlstm_mixer(hn, mlstm_w_in[j], mlstm_b_gate[j], mlstm_head_g[j], mlstm_w_out[j])
        else:
            mix = hgrn_mixer(hn, hgrn_w_in[j], lower_bounds[layer], hgrn_head_g[j], hgrn_w_out[j])
        h = h + mix
        h = h + conv_glu(rmsnorm(h, norm_ffn_g[layer]), ffn_w_up[layer], ffn_conv_w[layer],
                         ffn_conv_b[layer], ffn_w_down[layer])
    return rmsnorm(h, final_g)
```

```python
import functools

import numpy as np
import jax
import jax.numpy as jnp
from jax import lax
from jax.experimental import pallas as pl
from jax.experimental.pallas import tpu as pltpu

F32 = jnp.float32
BF16 = jnp.bfloat16
HIGHEST = lax.Precision.HIGHEST

EPS = 1e-6
D_MODEL = 2048
LANES = 128
SUBLANES = 8
VMEM_LIMIT_BYTES = 56 * 1024 * 1024

A_HEADS, A_DK, A_DV = 8, 128, 256
A_CHUNK = 256
B_HEADS, B_DK, B_DV = 16, 128, 128
B_CHUNK = 128
B_LEVELS = 7
D_FF = 5504
FF_TILE = 256
D_FF_PAD = 5632
INV_SQRT2 = 0.7071067811865476

_NT = (((1,), (1,)), ((), ()))


def _params(sem):
    return pltpu.CompilerParams(dimension_semantics=sem, vmem_limit_bytes=VMEM_LIMIT_BYTES)


def _rms_rows(x, g):
    ms = jnp.mean(x * x, axis=-1, keepdims=True)
    return x * lax.rsqrt(ms + EPS) * g


def _sigmoid(x):
    return 1.0 / (1.0 + jnp.exp(-x))


def _log_sigmoid(x):
    return jnp.minimum(x, 0.0) - jnp.log1p(jnp.exp(-jnp.abs(x)))


def _norm_kernel(x_ref, g_ref, o_ref):
    o_ref[...] = _rms_rows(x_ref[...], g_ref[...]).astype(o_ref.dtype)


def rmsnorm(x, g, out_dtype, tm=256):
    t, d = x.shape
    return pl.pallas_call(
        _norm_kernel,
        grid=(t // tm,),
        in_specs=[pl.BlockSpec((tm, d), lambda i: (i, 0)), pl.BlockSpec((1, d), lambda i: (0, 0))],
        out_specs=pl.BlockSpec((tm, d), lambda i: (i, 0)),
        out_shape=jax.ShapeDtypeStruct((t, d), out_dtype),
        compiler_params=_params(("parallel",)),
        name="rmsnorm",
    )(x, g.reshape(1, d))


def _norm_proj_kernel(x_ref, g_ref, w_ref, *rest, row_chunk, has_gate):
    if has_gate:
        wg_ref, o_ref, og_ref, hn_ref = rest
    else:
        o_ref, hn_ref = rest

    @pl.when(pl.program_id(1) == 0)
    def _():
        def body(r, carry):
            rows = pl.ds(pl.multiple_of(r * row_chunk, row_chunk), row_chunk)
            hn_ref[rows, :] = _rms_rows(x_ref[rows, :], g_ref[...]).astype(BF16)
            return carry

        lax.fori_loop(0, x_ref.shape[0] // row_chunk, body, 0)
        if has_gate:
            og_ref[...] = jnp.dot(hn_ref[...], wg_ref[...], preferred_element_type=F32)

    o_ref[...] = jnp.dot(hn_ref[...], w_ref[...], preferred_element_type=F32).astype(o_ref.dtype)


def norm_proj(x, g, w, out_dtype, w_gate=None, tm=1024, tn=512):
    t, d = x.shape
    n = w.shape[1]
    has_gate = w_gate is not None
    in_specs = [
        pl.BlockSpec((tm, d), lambda i, j: (i, 0)),
        pl.BlockSpec((1, d), lambda i, j: (0, 0)),
        pl.BlockSpec((d, tn), lambda i, j: (0, j)),
    ]
    out_specs = [pl.BlockSpec((tm, tn), lambda i, j: (i, j))]
    out_shape = [jax.ShapeDtypeStruct((t, n), out_dtype)]
    args = [x, g.reshape(1, d), w]
    if has_gate:
        ng = w_gate.shape[1]
        in_specs.append(pl.BlockSpec((d, ng), lambda i, j: (0, 0)))
        out_specs.append(pl.BlockSpec((tm, ng), lambda i, j: (i, 0)))
        out_shape.append(jax.ShapeDtypeStruct((t, ng), F32))
        args.append(w_gate)
    outs = pl.pallas_call(
        functools.partial(_norm_proj_kernel, row_chunk=64, has_gate=has_gate),
        grid=(t // tm, n // tn),
        in_specs=in_specs,
        out_specs=out_specs,
        out_shape=out_shape,
        scratch_shapes=[pltpu.VMEM((tm, d), BF16)],
        compiler_params=_params(("parallel", "arbitrary")),
        name="norm_proj",
    )(*args)
    return outs if has_gate else outs[0]


def _mlstm_direction(q, k, v, ig_c, lf_c, ig_r, lf_r, tri, ct_ref, n_ref, m_ref, reverse):
    L = q.shape[0]
    keep = tri > 0.5
    b_c = jnp.dot(tri, jnp.broadcast_to(lf_c, (L, LANES)), precision=HIGHEST,
                  preferred_element_type=F32)
    b_r = lax.dot_general(jnp.broadcast_to(lf_r, (SUBLANES, L)), tri, _NT, precision=HIGHEST,
                          preferred_element_type=F32)
    bc1 = b_c[:, :1]
    m_st = m_ref[...]
    d = jnp.where(keep, bc1 - b_r[:1, :] + ig_r, -jnp.inf)
    m_inter = bc1 + m_st[:, :1]
    m_t = jnp.maximum(jnp.max(d, axis=-1, keepdims=True), m_inter)
    scale = A_DK ** -0.5
    qk = lax.dot_general(q, k, _NT, preferred_element_type=F32)
    s = qk * jnp.exp(d - m_t) * scale
    w_inter = jnp.exp(m_inter - m_t) * scale
    num = (jnp.dot(s.astype(BF16), v, preferred_element_type=F32)
           + w_inter * jnp.dot(q, ct_ref[...].astype(BF16), preferred_element_type=F32))
    den = (jnp.sum(s, axis=-1, keepdims=True)
           + w_inter * jnp.sum(q.astype(F32) * n_ref[...], axis=-1, keepdims=True))
    h = num / jnp.maximum(jnp.abs(den), jnp.exp(-m_t))

    g_tot = b_c[0:1, :] if reverse else b_c[L - 1:L, :]
    a = g_tot - b_c + jnp.broadcast_to(ig_c, (L, LANES))
    m_new = jnp.maximum(g_tot + m_st, jnp.max(a, axis=0, keepdims=True))
    w_s = jnp.exp(a - m_new)
    decay = jnp.exp(g_tot + m_st - m_new)
    kw = k.astype(F32) * w_s
    ct_ref[...] = decay[:, :1] * ct_ref[...] + jnp.dot(kw.T.astype(BF16), v, preferred_element_type=F32)
    n_ref[...] = decay * n_ref[...] + jnp.sum(kw, axis=0, keepdims=True)
    m_ref[...] = m_new
    return h


def _mlstm_kernel(bias_ref, qf_ref, kf_ref, vf_ref, gcf_ref, grf_ref,
                  qb_ref, kb_ref, vb_ref, gcb_ref, grb_ref, trif_ref, trib_ref,
                  yf_ref, yb_ref, ctf, nf, mf, ctb, nb, mb):
    head = pl.program_id(1)

    @pl.when(pl.program_id(2) == 0)
    def _():
        for ref in (ctf, nf, mf, ctb, nb, mb):
            ref[...] = jnp.zeros_like(ref)

    lane = lax.broadcasted_iota(jnp.int32, (1, LANES), 1)

    def gate_col(gc_ref, idx):
        col = jnp.sum(jnp.where(lane == idx, gc_ref[0], 0.0), axis=-1, keepdims=True)
        return col + bias_ref[idx]

    def gate_row(gr_ref, idx):
        return gr_ref[0, pl.ds(idx, 1), :] + bias_ref[idx]

    def run(q_ref, k_ref, v_ref, gc_ref, gr_ref, tri_ref, y_ref, state, i_idx, f_idx, reverse):
        h = _mlstm_direction(
            q_ref[...].astype(BF16), k_ref[...].astype(BF16), v_ref[...].astype(BF16),
            gate_col(gc_ref, i_idx), _log_sigmoid(gate_col(gc_ref, f_idx)),
            gate_row(gr_ref, i_idx), _log_sigmoid(gate_row(gr_ref, f_idx)),
            tri_ref[...], *state, reverse)
        y_ref[...] = h

    run(qf_ref, kf_ref, vf_ref, gcf_ref, grf_ref, trif_ref, yf_ref, (ctf, nf, mf),
        head, A_HEADS + head, False)
    run(qb_ref, kb_ref, vb_ref, gcb_ref, grb_ref, trib_ref, yb_ref, (ctb, nb, mb),
        2 * A_HEADS + head, 3 * A_HEADS + head, True)


def mlstm_scan(p, gates, bias, batch, seq):
    L = A_CHUNK
    nc = seq // L
    t = batch * seq
    g_col = gates.reshape(batch, seq, LANES)
    g_row = jnp.transpose(g_col, (0, 2, 1))
    tri_f = jnp.asarray(np.tril(np.ones((L, L), np.float32)))
    tri_b = jnp.asarray(np.triu(np.ones((L, L), np.float32)))
    kcol = (A_HEADS * A_DK) // A_DK
    vcol = (2 * A_HEADS * A_DK) // A_DV

    fwd = lambda b, h, c: b * nc + c
    bwd = lambda b, h, c: b * nc + (nc - 1 - c)

    def dir_specs(row):
        return [
            pl.BlockSpec((L, A_DK), lambda b, h, c: (row(b, h, c), h)),
            pl.BlockSpec((L, A_DK), lambda b, h, c: (row(b, h, c), kcol + h)),
            pl.BlockSpec((L, A_DV), lambda b, h, c: (row(b, h, c), vcol + h)),
            pl.BlockSpec((1, L, LANES), lambda b, h, c: (b, row(b, h, c) - b * nc, 0)),
            pl.BlockSpec((1, LANES, L), lambda b, h, c: (b, 0, row(b, h, c) - b * nc)),
        ]

    const = pl.BlockSpec((L, L), lambda b, h, c: (0, 0))
    in_specs = ([pl.BlockSpec(memory_space=pltpu.SMEM)] + dir_specs(fwd) + dir_specs(bwd) + [const, const])
    out_specs = [
        pl.BlockSpec((L, A_DV), lambda b, h, c: (fwd(b, h, c), h)),
        pl.BlockSpec((L, A_DV), lambda b, h, c: (bwd(b, h, c), h)),
    ]
    state = [pltpu.VMEM((A_DK, A_DV), F32), pltpu.VMEM((1, A_DK), F32), pltpu.VMEM((1, LANES), F32)]
    return pl.pallas_call(
        _mlstm_kernel,
        grid=(batch, A_HEADS, nc),
        in_specs=in_specs,
        out_specs=out_specs,
        out_shape=[jax.ShapeDtypeStruct((t, A_HEADS * A_DV), F32)] * 2,
        scratch_shapes=state + state,
        compiler_params=_params(("parallel", "parallel", "arbitrary")),
        name="mlstm_scan",
    )(bias, p, p, p, g_col, g_row, p, p, p, g_col, g_row, tri_f, tri_b)


def _boundary_rows(b, c, r_off):
    L = b.shape[0]
    if c >= SUBLANES:
        b3 = b.reshape(L // c, c, LANES)
        return jnp.broadcast_to(b3[:, r_off:r_off + 1, :], (L // c, c, LANES)).reshape(L, LANES)
    pos = lax.broadcasted_iota(jnp.int32, b.shape, 0) & (c - 1)
    out = None
    for i in range(c):
        shift = (i - r_off) % L
        rolled = b if shift == 0 else pltpu.roll(b, shift, 0)
        out = rolled if out is None else jnp.where(pos == i, rolled, out)
    return out


def _hgrn_direction(q, v, f, lb, tri, masks_ref, st_ref, reverse):
    L = q.shape[0]
    e = jnp.exp(-jnp.abs(f))
    r = 1.0 / (1.0 + e)
    er = e * r
    pos = f >= 0.0
    lf = jnp.log(lb + (1.0 - lb) * jnp.where(pos, r, er))
    kk = (1.0 - lb) * jnp.where(pos, er, r)
    b = jnp.dot(tri, lf, precision=HIGHEST, preferred_element_type=F32)
    b_tot = b[0:1, :] if reverse else b[L - 1:L, :]
    st = st_ref[...]
    v16 = v.astype(BF16)

    o = lax.dot_general((q * jnp.exp(b)).astype(BF16), st.astype(BF16), _NT, preferred_element_type=F32)

    row = lax.broadcasted_iota(jnp.int32, (L, LANES), 0)
    attn = jnp.zeros((L, L), F32)
    for j in range(B_LEVELS):
        half = 1 << j
        b_mid = _boundary_rows(b, 2 * half, half if reverse else half - 1)
        is_query = ((row >> j) & 1) == (0 if reverse else 1)
        w = jnp.exp(jnp.where(is_query, b - b_mid, b_mid - b))
        pj = lax.dot_general((q * w).astype(BF16), (kk * w).astype(BF16), _NT, preferred_element_type=F32)
        attn = attn + pj * masks_ref[j]
    o = o + jnp.dot(attn.astype(BF16), v16, preferred_element_type=F32)
    o = o + jnp.sum(q * kk, axis=-1, keepdims=True) * v

    kt = (kk * jnp.exp(b_tot - b)).astype(BF16)
    st_ref[...] = jnp.exp(b_tot) * st + jnp.dot(v.T.astype(BF16), kt, preferred_element_type=F32)
    return o


def _hgrn_kernel(lbp_ref, qf_ref, vf_ref, ff_ref, qb_ref, vb_ref, fb_ref,
                 trif_ref, trib_ref, maskf_ref, maskb_ref, yf_ref, yb_ref, stf, stb, *, layer):
    @pl.when(pl.program_id(2) == 0)
    def _():
        stf[...] = jnp.zeros_like(stf)
        stb[...] = jnp.zeros_like(stb)

    lbp = lbp_ref[...]
    ex = jnp.exp(lbp - jnp.max(lbp, axis=0, keepdims=True))
    sm = ex / jnp.sum(ex, axis=0, keepdims=True)
    lb = jnp.sum(sm[0:layer + 1, :], axis=0, keepdims=True) - sm[0:1, :]

    yf_ref[...] = _hgrn_direction(qf_ref[...].astype(F32), vf_ref[...].astype(F32), ff_ref[...].astype(F32),
                                  lb, trif_ref[...], maskf_ref, stf, False)
    yb_ref[...] = _hgrn_direction(qb_ref[...].astype(F32), vb_ref[...].astype(F32), fb_ref[...].astype(F32),
                                  lb, trib_ref[...], maskb_ref, stb, True)


def _hgrn_masks(L):
    t = np.arange(L)[:, None]
    s = np.arange(L)[None, :]
    out = []
    for j in range(B_LEVELS):
        half = 1 << j
        c = 2 * half
        out.append(((t // c == s // c) & (t % c >= half) & (s % c < half)).astype(np.float32))
    return np.stack(out)


def hgrn_scan(p, lb_param, layer, batch, seq):
    L = B_CHUNK
    nc = seq // L
    t = batch * seq
    nh = B_HEADS
    tri_f = jnp.asarray(np.tril(np.ones((L, L), np.float32)))
    tri_b = jnp.asarray(np.triu(np.ones((L, L), np.float32)))
    masks_f = _hgrn_masks(L)
    masks_b = np.ascontiguousarray(np.transpose(masks_f, (0, 2, 1)))
    depth = lb_param.shape[0]

    fwd = lambda b, h, c: b * nc + c
    bwd = lambda b, h, c: b * nc + (nc - 1 - c)

    def dir_specs(row, fcol):
        return [
            pl.BlockSpec((L, B_DK), lambda b, h, c: (row(b, h, c), h)),
            pl.BlockSpec((L, B_DV), lambda b, h, c: (row(b, h, c), nh + h)),
            pl.BlockSpec((L, B_DK), lambda b, h, c: (row(b, h, c), fcol * nh + h)),
        ]

    const2 = pl.BlockSpec((L, L), lambda b, h, c: (0, 0))
    const3 = pl.BlockSpec((B_LEVELS, L, L), lambda b, h, c: (0, 0, 0))
    in_specs = ([pl.BlockSpec((depth, B_DK), lambda b, h, c: (0, h))]
                + dir_specs(fwd, 3) + dir_specs(bwd, 4) + [const2, const2, const3, const3])
    out_specs = [
        pl.BlockSpec((L, B_DV), lambda b, h, c: (fwd(b, h, c), h)),
        pl.BlockSpec((L, B_DV), lambda b, h, c: (bwd(b, h, c), h)),
    ]
    return pl.pallas_call(
        functools.partial(_hgrn_kernel, layer=layer),
        grid=(batch, nh, nc),
        in_specs=in_specs,
        out_specs=out_specs,
        out_shape=[jax.ShapeDtypeStruct((t, nh * B_DV), F32)] * 2,
        scratch_shapes=[pltpu.VMEM((B_DV, B_DK), F32)] * 2,
        compiler_params=_params(("parallel", "parallel", "arbitrary")),
        name="hgrn_scan",
    )(lb_param, p, p, p, p, p, p, tri_f, tri_b, jnp.asarray(masks_f), jnp.asarray(masks_b))


def _mix_out_kernel(yf_ref, yb_ref, gate_ref, hg_ref, res_ref, w_ref, out_ref, z_ref, *, n_heads, silu):
    @pl.when(pl.program_id(1) == 0)
    def _():
        dh = z_ref.shape[1] // n_heads
        for hh in range(n_heads):
            sl = slice(hh * dh, (hh + 1) * dh)
            y = yf_ref[:, sl] + yb_ref[:, sl]
            ms = jnp.mean(y * y, axis=-1, keepdims=True)
            gate = gate_ref[:, sl].astype(F32)
            act = _sigmoid(gate)
            if silu:
                act = gate * act
            z_ref[:, sl] = (y * lax.rsqrt(ms + EPS) * hg_ref[:, sl] * act).astype(BF16)

    out_ref[...] = res_ref[...] + jnp.dot(z_ref[...], w_ref[...], preferred_element_type=F32)


def mix_out(yf, yb, p, gate_col_block, head_g, res, w_out, n_heads, silu, tm=512, tn=512):
    t, d = yf.shape
    return pl.pallas_call(
        functools.partial(_mix_out_kernel, n_heads=n_heads, silu=silu),
        grid=(t // tm, d // tn),
        in_specs=[
            pl.BlockSpec((tm, d), lambda i, j: (i, 0)),
            pl.BlockSpec((tm, d), lambda i, j: (i, 0)),
            pl.BlockSpec((tm, d), lambda i, j: (i, gate_col_block)),
            pl.BlockSpec((1, d), lambda i, j: (0, 0)),
            pl.BlockSpec((tm, tn), lambda i, j: (i, j)),
            pl.BlockSpec((d, tn), lambda i, j: (0, j)),
        ],
        out_specs=pl.BlockSpec((tm, tn), lambda i, j: (i, j)),
        out_shape=jax.ShapeDtypeStruct((t, d), F32),
        scratch_shapes=[pltpu.VMEM((tm, d), BF16)],
        compiler_params=_params(("parallel", "arbitrary")),
        name="mix_out",
    )(yf, yb, p, head_g.reshape(1, d), res, w_out)


def _glu_up_kernel(hn_ref, wa_ref, wv_ref, cw_ref, cb_ref, g_ref, a_scr, v_scr, *, row_chunk):
    s = hn_ref.shape[0]
    pad = SUBLANES
    hn = hn_ref[...]
    a_scr[0:pad, :] = jnp.zeros((pad, a_scr.shape[1]), F32)
    a_scr[pad + s:2 * pad + s, :] = jnp.zeros((pad, a_scr.shape[1]), F32)
    a_scr[pad:pad + s, :] = jnp.dot(hn, wa_ref[...], preferred_element_type=F32)
    v_scr[...] = jnp.dot(hn, wv_ref[...], preferred_element_type=F32)
    for r in range(s // row_chunk):
        lo = pad + r * row_chunk
        c = (cb_ref[...]
             + a_scr[lo - 1:lo - 1 + row_chunk, :] * cw_ref[0:1, :]
             + a_scr[lo:lo + row_chunk, :] * cw_ref[1:2, :]
             + a_scr[lo + 1:lo + 1 + row_chunk, :] * cw_ref[2:3, :])
        gelu = 0.5 * c * (1.0 + lax.erf(c * INV_SQRT2))
        g_ref[r * row_chunk:(r + 1) * row_chunk, :] = (
            gelu * v_scr[r * row_chunk:(r + 1) * row_chunk, :]).astype(BF16)


def glu_up(hn, w_a, w_v, conv_w, conv_b, batch, seq):
    t, d = hn.shape
    nf = w_a.shape[1]
    tf = FF_TILE
    return pl.pallas_call(
        functools.partial(_glu_up_kernel, row_chunk=256),
        grid=(batch, nf // tf),
        in_specs=[
            pl.BlockSpec((seq, d), lambda b, j: (b, 0)),
            pl.BlockSpec((d, tf), lambda b, j: (0, j)),
            pl.BlockSpec((d, tf), lambda b, j: (0, j)),
            pl.BlockSpec((3, tf), lambda b, j: (0, j)),
            pl.BlockSpec((1, tf), lambda b, j: (0, j)),
        ],
        out_specs=pl.BlockSpec((seq, tf), lambda b, j: (b, j)),
        out_shape=jax.ShapeDtypeStruct((t, nf), BF16),
        scratch_shapes=[pltpu.VMEM((seq + 2 * SUBLANES, tf), F32), pltpu.VMEM((seq, tf), F32)],
        compiler_params=_params(("parallel", "arbitrary")),
        name="glu_up",
    )(hn, w_a, w_v, conv_w, conv_b)


def _proj_res_kernel(x_ref, w_ref, res_ref, out_ref):
    out_ref[...] = res_ref[...] + jnp.dot(x_ref[...], w_ref[...], preferred_element_type=F32)


def proj_res(x, w, res, tm=1024, tn=512):
    t, k = x.shape
    n = w.shape[1]
    return pl.pallas_call(
        _proj_res_kernel,
        grid=(t // tm, n // tn),
        in_specs=[
            pl.BlockSpec((tm, k), lambda i, j: (i, 0)),
            pl.BlockSpec((k, tn), lambda i, j: (0, j)),
            pl.BlockSpec((tm, tn), lambda i, j: (i, j)),
        ],
        out_specs=pl.BlockSpec((tm, tn), lambda i, j: (i, j)),
        out_shape=jax.ShapeDtypeStruct((t, n), F32),
        compiler_params=_params(("parallel", "arbitrary")),
        name="proj_res",
    )(x, w, res)


def conv_glu(h, norm_g, w_up, conv_w, conv_b, w_down, batch, seq):
    fpad = D_FF_PAD - D_FF
    w_a = jnp.pad(w_up[:, :D_FF], ((0, 0), (0, fpad))).astype(BF16)
    w_v = jnp.pad(w_up[:, D_FF:], ((0, 0), (0, fpad))).astype(BF16)
    cw = jnp.pad(conv_w, ((0, 0), (0, fpad)))
    cb = jnp.pad(conv_b, (0, fpad)).reshape(1, D_FF_PAD)
    wd = jnp.pad(w_down, ((0, fpad), (0, 0))).astype(BF16)
    hn = rmsnorm(h, norm_g, BF16)
    g = glu_up(hn, w_a, w_v, cw, cb, batch, seq)
    return proj_res(g, wd, h)


def kernel(x, norm_mix_g, norm_ffn_g, mlstm_w_in, mlstm_b_gate, mlstm_head_g, mlstm_w_out,
           hgrn_w_in, hgrn_lb, hgrn_head_g, hgrn_w_out, ffn_w_up, ffn_conv_w, ffn_conv_b,
           ffn_w_down, final_g):
    batch, seq, d = x.shape
    h = x.reshape(batch * seq, d)
    depth = norm_mix_g.shape[0]
    for layer in range(depth):
        j = layer // 2
        if layer % 2 == 0:
            n_main = 2 * A_HEADS * A_DK + 2 * A_HEADS * A_DV
            w_in = mlstm_w_in[j]
            n_gate = w_in.shape[1] - n_main
            w_gate = jnp.pad(w_in[:, n_main:], ((0, 0), (0, LANES - n_gate))).astype(BF16)
            p, gates = norm_proj(h, norm_mix_g[layer], w_in[:, :n_main].astype(BF16), F32, w_gate=w_gate)
            bias = jnp.pad(mlstm_b_gate[j], (0, LANES - n_gate))
            yf, yb = mlstm_scan(p, gates, bias, batch, seq)
            h = mix_out(yf, yb, p, 2, mlstm_head_g[j], h, mlstm_w_out[j].astype(BF16), A_HEADS, silu=False)
        else:
            p = norm_proj(h, norm_mix_g[layer], hgrn_w_in[j].astype(BF16), F32)
            yf, yb = hgrn_scan(p, hgrn_lb, layer, batch, seq)
            h = mix_out(yf, yb, p, 2, hgrn_head_g[j], h, hgrn_w_out[j].astype(BF16), B_HEADS, silu=True)
        h = conv_glu(h, norm_ffn_g[layer], ffn_w_up[layer], ffn_conv_w[layer], ffn_conv_b[layer],
                     ffn_w_down[layer], batch, seq)
    return rmsnorm(h, final_g, x.dtype).reshape(batch, seq, d)
```

```python
import functools

import numpy as np
import jax
import jax.numpy as jnp
from jax import lax
from jax.experimental import pallas as pl
from jax.experimental.pallas import tpu as pltpu

F32 = jnp.float32
BF16 = jnp.bfloat16
HIGHEST = lax.Precision.HIGHEST

EPS = 1e-6
D_MODEL = 2048
LANES = 128
SUBLANES = 8
VMEM_LIMIT_BYTES = 56 * 1024 * 1024

A_HEADS, A_DK, A_DV = 8, 128, 256
A_CHUNK = 256
B_HEADS, B_DK, B_DV = 16, 128, 128
B_CHUNK = 128
B_LEVELS = 7
D_FF = 5504
FF_TILE = 256
D_FF_PAD = 5632
INV_SQRT2 = 0.7071067811865476

_NT = (((1,), (1,)), ((), ()))


def _params(sem):
    return pltpu.CompilerParams(dimension_semantics=sem, vmem_limit_bytes=VMEM_LIMIT_BYTES)


def _rms_rows(x, g):
    ms = jnp.mean(x * x, axis=-1, keepdims=True)
    return x * lax.rsqrt(ms + EPS) * g


def _sigmoid(x):
    return 1.0 / (1.0 + jnp.exp(-x))


def _log_sigmoid(x):
    return jnp.minimum(x, 0.0) - jnp.log1p(jnp.exp(-jnp.abs(x)))


def _norm_kernel(x_ref, g_ref, o_ref):
    o_ref[...] = _rms_rows(x_ref[...], g_ref[...]).astype(o_ref.dtype)


def rmsnorm(x, g, out_dtype, tm=256):
    t, d = x.shape
    return pl.pallas_call(
        _norm_kernel,
        grid=(t // tm,),
        in_specs=[pl.BlockSpec((tm, d), lambda i: (i, 0)), pl.BlockSpec((1, d), lambda i: (0, 0))],
        out_specs=pl.BlockSpec((tm, d), lambda i: (i, 0)),
        out_shape=jax.ShapeDtypeStruct((t, d), out_dtype),
        compiler_params=_params(("parallel",)),
        name="rmsnorm",
    )(x, g.reshape(1, d))


def _norm_proj_kernel(x_ref, g_ref, w_ref, *rest, row_chunk, has_gate):
    if has_gate:
        wg_ref, o_ref, og_ref, hn_ref = rest
    else:
        o_ref, hn_ref = rest

    @pl.when(pl.program_id(1) == 0)
    def _():
        def body(r, carry):
            rows = pl.ds(pl.multiple_of(r * row_chunk, row_chunk), row_chunk)
            hn_ref[rows, :] = _rms_rows(x_ref[rows, :], g_ref[...]).astype(BF16)
            return carry

        lax.fori_loop(0, x_ref.shape[0] // row_chunk, body, 0)
        if has_gate:
            og_ref[...] = jnp.dot(hn_ref[...], wg_ref[...], preferred_element_type=F32)

    o_ref[...] = jnp.dot(hn_ref[...], w_ref[...], preferred_element_type=F32).astype(o_ref.dtype)


def norm_proj(x, g, w, out_dtype, w_gate=None, tm=1024, tn=512):
    t, d = x.shape
    n = w.shape[1]
    has_gate = w_gate is not None
    in_specs = [
        pl.BlockSpec((tm, d), lambda i, j: (i, 0)),
        pl.BlockSpec((1, d), lambda i, j: (0, 0)),
        pl.BlockSpec((d, tn), lambda i, j: (0, j)),
    ]
    out_specs = [pl.BlockSpec((tm, tn), lambda i, j: (i, j))]
    out_shape = [jax.ShapeDtypeStruct((t, n), out_dtype)]
    args = [x, g.reshape(1, d), w]
    if has_gate:
        ng = w_gate.shape[1]
        in_specs.append(pl.BlockSpec((d, ng), lambda i, j: (0, 0)))
        out_specs.append(pl.BlockSpec((tm, ng), lambda i, j: (i, 0)))
        out_shape.append(jax.ShapeDtypeStruct((t, ng), F32))
        args.append(w_gate)
    outs = pl.pallas_call(
        functools.partial(_norm_proj_kernel, row_chunk=64, has_gate=has_gate),
        grid=(t // tm, n // tn),
        in_specs=in_specs,
        out_specs=out_specs,
        out_shape=out_shape,
        scratch_shapes=[pltpu.VMEM((tm, d), BF16)],
        compiler_params=_params(("parallel", "arbitrary")),
        name="norm_proj",
    )(*args)
    return outs if has_gate else outs[0]


def _mlstm_direction(q, k, v, ig_c, lf_c, ig_r, lf_r, tri, ct_ref, n_ref, m_ref, reverse):
    L = q.shape[0]
    keep = tri > 0.5
    b_c = jnp.dot(tri, jnp.broadcast_to(lf_c, (L, LANES)), precision=HIGHEST,
                  preferred_element_type=F32)
    b_r = lax.dot_general(jnp.broadcast_to(lf_r, (SUBLANES, L)), tri, _NT, precision=HIGHEST,
                          preferred_element_type=F32)
    bc1 = b_c[:, :1]
    m_st = m_ref[...]
    d = jnp.where(keep, bc1 - b_r[:1, :] + ig_r, -jnp.inf)
    m_inter = bc1 + m_st[:, :1]
    m_t = jnp.maximum(jnp.max(d, axis=-1, keepdims=True), m_inter)
    scale = A_DK ** -0.5
    qk = lax.dot_general(q, k, _NT, preferred_element_type=F32)
    s = qk * jnp.exp(d - m_t) * scale
    w_inter = jnp.exp(m_inter - m_t) * scale
    num = (jnp.dot(s.astype(BF16), v, preferred_element_type=F32)
           + w_inter * jnp.dot(q, ct_ref[...].astype(BF16), preferred_element_type=F32))
    den = (jnp.sum(s, axis=-1, keepdims=True)
           + w_inter * jnp.sum(q.astype(F32) * n_ref[...], axis=-1, keepdims=True))
    h = num / jnp.maximum(jnp.abs(den), jnp.exp(-m_t))

    g_tot = b_c[0:1, :] if reverse else b_c[L - 1:L, :]
    a = g_tot - b_c + jnp.broadcast_to(ig_c, (L, LANES))
    m_new = jnp.maximum(g_tot + m_st, jnp.max(a, axis=0, keepdims=True))
    w_s = jnp.exp(a - m_new)
    decay = jnp.exp(g_tot + m_st - m_new)
    kw = k.astype(F32) * w_s
    ct_ref[...] = decay[:, :1] * ct_ref[...] + jnp.dot(kw.T.astype(BF16), v, preferred_element_type=F32)
    n_ref[...] = decay * n_ref[...] + jnp.sum(kw, axis=0, keepdims=True)
    m_ref[...] = m_new
    return h


def _mlstm_kernel(bias_ref, qf_ref, kf_ref, vf_ref, gcf_ref, grf_ref,
                  qb_ref, kb_ref, vb_ref, gcb_ref, grb_ref, trif_ref, trib_ref,
                  yf_ref, yb_ref, ctf, nf, mf, ctb, nb, mb):
    head = pl.program_id(1)

    @pl.when(pl.program_id(2) == 0)
    def _():
        for ref in (ctf, nf, mf, ctb, nb, mb):
            ref[...] = jnp.zeros_like(ref)

    lane = lax.broadcasted_iota(jnp.int32, (1, LANES), 1)

    def gate_col(gc_ref, idx):
        col = jnp.sum(jnp.where(lane == idx, gc_ref[0], 0.0), axis=-1, keepdims=True)
        return col + bias_ref[idx]

    def gate_row(gr_ref, idx):
        return gr_ref[0, pl.ds(idx, 1), :] + bias_ref[idx]

    def run(q_ref, k_ref, v_ref, gc_ref, gr_ref, tri_ref, y_ref, state, i_idx, f_idx, reverse):
        h = _mlstm_direction(
            q_ref[...].astype(BF16), k_ref[...].astype(BF16), v_ref[...].astype(BF16),
            gate_col(gc_ref, i_idx), _log_sigmoid(gate_col(gc_ref, f_idx)),
            gate_row(gr_ref, i_idx), _log_sigmoid(gate_row(gr_ref, f_idx)),
            tri_ref[...], *state, reverse)
        y_ref[...] = h

    run(qf_ref, kf_ref, vf_ref, gcf_ref, grf_ref, trif_ref, yf_ref, (ctf, nf, mf),
        head, A_HEADS + head, False)
    run(qb_ref, kb_ref, vb_ref, gcb_ref, grb_ref, trib_ref, yb_ref, (ctb, nb, mb),
        2 * A_HEADS + head, 3 * A_HEADS + head, True)


def mlstm_scan(p, gates, bias, batch, seq):
    L = A_CHUNK
    nc = seq // L
    t = batch * seq
    g_col = gates.reshape(batch, seq, LANES)
    g_row = jnp.transpose(g_col, (0, 2, 1))
    tri_f = jnp.asarray(np.tril(np.ones((L, L), np.float32)))
    tri_b = jnp.asarray(np.triu(np.ones((L, L), np.float32)))
    kcol = (A_HEADS * A_DK) // A_DK
    vcol = (2 * A_HEADS * A_DK) // A_DV

    fwd = lambda b, h, c: b * nc + c
    bwd = lambda b, h, c: b * nc + (nc - 1 - c)

    def dir_specs(row):
        return [
            pl.BlockSpec((L, A_DK), lambda b, h, c: (row(b, h, c), h)),
            pl.BlockSpec((L, A_DK), lambda b, h, c: (row(b, h, c), kcol + h)),
            pl.BlockSpec((L, A_DV), lambda b, h, c: (row(b, h, c), vcol + h)),
            pl.BlockSpec((1, L, LANES), lambda b, h, c: (b, row(b, h, c) - b * nc, 0)),
            pl.BlockSpec((1, LANES, L), lambda b, h, c: (b, 0, row(b, h, c) - b * nc)),
        ]

    const = pl.BlockSpec((L, L), lambda b, h, c: (0, 0))
    in_specs = ([pl.BlockSpec(memory_space=pltpu.SMEM)] + dir_specs(fwd) + dir_specs(bwd) + [const, const])
    out_specs = [
        pl.BlockSpec((L, A_DV), lambda b, h, c: (fwd(b, h, c), h)),
        pl.BlockSpec((L, A_DV), lambda b, h, c: (bwd(b, h, c), h)),
    ]
    state = [pltpu.VMEM((A_DK, A_DV), F32), pltpu.VMEM((1, A_DK), F32), pltpu.VMEM((1, LANES), F32)]
    return pl.pallas_call(
        _mlstm_kernel,
        grid=(batch, A_HEADS, nc),
        in_specs=in_specs,
        out_specs=out_specs,
        out_shape=[jax.ShapeDtypeStruct((t, A_HEADS * A_DV), F32)] * 2,
        scratch_shapes=state + state,
        compiler_params=_params(("parallel", "parallel", "arbitrary")),
        name="mlstm_scan",
    )(bias, p, p, p, g_col, g_row, p, p, p, g_col, g_row, tri_f, tri_b)


def _mid_rows(b, c, r_off):
    L = b.shape[0]
    if c >= SUBLANES:
        b3 = b.reshape(L // c, c, LANES)
        return jnp.broadcast_to(b3[:, r_off:r_off + 1, :], b3.shape).reshape(L, LANES)
    b3 = b.reshape(L // SUBLANES, SUBLANES, LANES)
    lo = jnp.broadcast_to(b3[:, r_off:r_off + 1, :], b3.shape)
    hi = jnp.broadcast_to(b3[:, c + r_off:c + r_off + 1, :], b3.shape)
    sub = lax.broadcasted_iota(jnp.int32, b3.shape, 1)
    return jnp.where(sub < c, lo, hi).reshape(L, LANES)


def _split3(x):
    hi = x.astype(BF16)
    r1 = x - hi.astype(F32)
    mid = r1.astype(BF16)
    lo = (r1 - mid.astype(F32)).astype(BF16)
    return hi, mid, lo


class _Stream:
    def __init__(self, q, v, f, lb, tri3, masks_ref, st_ref, reverse):
        self.q, self.v, self.f, self.lb = q, v, f, lb
        self.tri3, self.masks_ref, self.st_ref, self.reverse = tri3, masks_ref, st_ref, reverse


def _hgrn_chunk(streams):
    L = streams[0].q.shape[0]
    row = lax.broadcasted_iota(jnp.int32, (L, LANES), 0)
    odd = (row & 1) == 1

    for s in streams:
        e = jnp.exp(-jnp.abs(s.f))
        r = 1.0 / (1.0 + e)
        er = e * r
        pos = s.f >= 0.0
        s.forget = s.lb + (1.0 - s.lb) * jnp.where(pos, r, er)
        s.kk = (1.0 - s.lb) * jnp.where(pos, er, r)
    for s in streams:
        s.b = jnp.dot(s.tri3, jnp.concatenate(_split3(jnp.log2(s.forget)), axis=0),
                      preferred_element_type=F32)
        s.b_tot = s.b[0:1, :] if s.reverse else s.b[L - 1:L, :]
    for s in streams:
        s.st = s.st_ref[...]
        s.o = lax.dot_general((s.q * jnp.exp2(s.b)).astype(BF16), s.st.astype(BF16), _NT,
                              preferred_element_type=F32)
        s.attn = None

    for j in range(B_LEVELS):
        half = 1 << j
        for s in streams:
            if j == 0:
                w = jnp.where(odd != s.reverse, s.forget, 1.0)
            else:
                b_mid = _mid_rows(s.b, 2 * half, half if s.reverse else half - 1)
                w = jnp.exp2(-jnp.abs(s.b - b_mid))
            pj = lax.dot_general((s.q * w).astype(BF16), (s.kk * w).astype(BF16), _NT,
                                 preferred_element_type=F32)
            pj = pj * s.masks_ref[j]
            s.attn = pj if s.attn is None else s.attn + pj
    for s in streams:
        s.o = s.o + jnp.dot(s.attn.astype(BF16), s.v.astype(BF16), preferred_element_type=F32)
        s.o = s.o + jnp.sum(s.q * s.kk, axis=-1, keepdims=True) * s.v
    for s in streams:
        kt = (s.kk * jnp.exp2(s.b_tot - s.b)).astype(BF16)
        s.st_ref[...] = jnp.exp2(s.b_tot) * s.st + jnp.dot(s.v.T.astype(BF16), kt, preferred_element_type=F32)


def _hgrn_kernel(lbp_ref, qf_ref, vf_ref, ff_ref, qb_ref, vb_ref, fb_ref,
                 trif_ref, trib_ref, maskf_ref, maskb_ref, yf_ref, yb_ref, stf, stb, *, layer, heads):
    @pl.when(pl.program_id(2) == 0)
    def _():
        stf[...] = jnp.zeros_like(stf)
        stb[...] = jnp.zeros_like(stb)

    lbp = lbp_ref[...]
    ex = jnp.exp(lbp - jnp.max(lbp, axis=0, keepdims=True))
    sm = ex / jnp.sum(ex, axis=0, keepdims=True)
    lb = jnp.sum(sm[0:layer + 1, :], axis=0, keepdims=True) - sm[0:1, :]

    tri_f, tri_b = trif_ref[...], trib_ref[...]
    streams = []
    for hh in range(heads):
        ks = slice(hh * B_DK, (hh + 1) * B_DK)
        vs = slice(hh * B_DV, (hh + 1) * B_DV)
        streams.append(_Stream(qf_ref[:, ks].astype(F32), vf_ref[:, vs].astype(F32), ff_ref[:, ks].astype(F32),
                               lb[:, ks], tri_f, maskf_ref, stf.at[hh], False))
        streams.append(_Stream(qb_ref[:, ks].astype(F32), vb_ref[:, vs].astype(F32), fb_ref[:, ks].astype(F32),
                               lb[:, ks], tri_b, maskb_ref, stb.at[hh], True))
    _hgrn_chunk(streams)
    for hh in range(heads):
        vs = slice(hh * B_DV, (hh + 1) * B_DV)
        yf_ref[:, vs] = streams[2 * hh].o
        yb_ref[:, vs] = streams[2 * hh + 1].o


def _hgrn_masks(L):
    t = np.arange(L)[:, None]
    s = np.arange(L)[None, :]
    out = []
    for j in range(B_LEVELS):
        half = 1 << j
        c = 2 * half
        out.append(((t // c == s // c) & (t % c >= half) & (s % c < half)).astype(np.float32))
    return np.stack(out)


def hgrn_scan(p, lb_param, layer, batch, seq, heads=4):
    L = B_CHUNK
    nc = seq // L
    t = batch * seq
    ng = B_HEADS // heads
    tri_f = jnp.asarray(np.tile(np.tril(np.ones((L, L), np.float32)), (1, 3)), dtype=BF16)
    tri_b = jnp.asarray(np.tile(np.triu(np.ones((L, L), np.float32)), (1, 3)), dtype=BF16)
    masks_f = _hgrn_masks(L)
    masks_b = np.ascontiguousarray(np.transpose(masks_f, (0, 2, 1)))
    depth = lb_param.shape[0]

    fwd = lambda b, h, c: b * nc + c
    bwd = lambda b, h, c: b * nc + (nc - 1 - c)

    def dir_specs(row, fcol):
        return [
            pl.BlockSpec((L, heads * B_DK), lambda b, h, c: (row(b, h, c), h)),
            pl.BlockSpec((L, heads * B_DV), lambda b, h, c: (row(b, h, c), ng + h)),
            pl.BlockSpec((L, heads * B_DK), lambda b, h, c: (row(b, h, c), fcol * ng + h)),
        ]

    const2 = pl.BlockSpec((L, 3 * L), lambda b, h, c: (0, 0))
    const3 = pl.BlockSpec((B_LEVELS, L, L), lambda b, h, c: (0, 0, 0))
    in_specs = ([pl.BlockSpec((depth, heads * B_DK), lambda b, h, c: (0, h))]
                + dir_specs(fwd, 3) + dir_specs(bwd, 4) + [const2, const2, const3, const3])
    out_specs = [
        pl.BlockSpec((L, heads * B_DV), lambda b, h, c: (fwd(b, h, c), h)),
        pl.BlockSpec((L, heads * B_DV), lambda b, h, c: (bwd(b, h, c), h)),
    ]
    return pl.pallas_call(
        functools.partial(_hgrn_kernel, layer=layer, heads=heads),
        grid=(batch, ng, nc),
        in_specs=in_specs,
        out_specs=out_specs,
        out_shape=[jax.ShapeDtypeStruct((t, B_HEADS * B_DV), F32)] * 2,
        scratch_shapes=[pltpu.VMEM((heads, B_DV, B_DK), F32)] * 2,
        compiler_params=_params(("parallel", "parallel", "arbitrary")),
        name="hgrn_scan",
    )(lb_param, p, p, p, p, p, p, tri_f, tri_b, jnp.asarray(masks_f), jnp.asarray(masks_b))


def _mix_out_kernel(yf_ref, yb_ref, gate_ref, hg_ref, res_ref, w_ref, out_ref, z_ref, *, n_heads, silu):
    @pl.when(pl.program_id(1) == 0)
    def _():
        dh = z_ref.shape[1] // n_heads
        for hh in range(n_heads):
            sl = slice(hh * dh, (hh + 1) * dh)
            y = yf_ref[:, sl] + yb_ref[:, sl]
            ms = jnp.mean(y * y, axis=-1, keepdims=True)
            gate = gate_ref[:, sl].astype(F32)
            act = _sigmoid(gate)
            if silu:
                act = gate * act
            z_ref[:, sl] = (y * lax.rsqrt(ms + EPS) * hg_ref[:, sl] * act).astype(BF16)

    out_ref[...] = res_ref[...] + jnp.dot(z_ref[...], w_ref[...], preferred_element_type=F32)


def mix_out(yf, yb, p, gate_col_block, head_g, res, w_out, n_heads, silu, tm=512, tn=512):
    t, d = yf.shape
    return pl.pallas_call(
        functools.partial(_mix_out_kernel, n_heads=n_heads, silu=silu),
        grid=(t // tm, d // tn),
        in_specs=[
            pl.BlockSpec((tm, d), lambda i, j: (i, 0)),
            pl.BlockSpec((tm, d), lambda i, j: (i, 0)),
            pl.BlockSpec((tm, d), lambda i, j: (i, gate_col_block)),
            pl.BlockSpec((1, d), lambda i, j: (0, 0)),
            pl.BlockSpec((tm, tn), lambda i, j: (i, j)),
            pl.BlockSpec((d, tn), lambda i, j: (0, j)),
        ],
        out_specs=pl.BlockSpec((tm, tn), lambda i, j: (i, j)),
        out_shape=jax.ShapeDtypeStruct((t, d), F32),
        scratch_shapes=[pltpu.VMEM((tm, d), BF16)],
        compiler_params=_params(("parallel", "arbitrary")),
        name="mix_out",
    )(yf, yb, p, head_g.reshape(1, d), res, w_out)


def _glu_up_kernel(hn_ref, wa_ref, wv_ref, cw_ref, cb_ref, g_ref, a_scr, v_scr, *, row_chunk):
    s = hn_ref.shape[0]
    pad = SUBLANES
    hn = hn_ref[...]
    a_scr[0:pad, :] = jnp.zeros((pad, a_scr.shape[1]), F32)
    a_scr[pad + s:2 * pad + s, :] = jnp.zeros((pad, a_scr.shape[1]), F32)
    a_scr[pad:pad + s, :] = jnp.dot(hn, wa_ref[...], preferred_element_type=F32)
    v_scr[...] = jnp.dot(hn, wv_ref[...], preferred_element_type=F32)
    for r in range(s // row_chunk):
        lo = pad + r * row_chunk
        c = (cb_ref[...]
             + a_scr[lo - 1:lo - 1 + row_chunk, :] * cw_ref[0:1, :]
             + a_scr[lo:lo + row_chunk, :] * cw_ref[1:2, :]
             + a_scr[lo + 1:lo + 1 + row_chunk, :] * cw_ref[2:3, :])
        gelu = 0.5 * c * (1.0 + lax.erf(c * INV_SQRT2))
        g_ref[r * row_chunk:(r + 1) * row_chunk, :] = (
            gelu * v_scr[r * row_chunk:(r + 1) * row_chunk, :]).astype(BF16)


def glu_up(hn, w_a, w_v, conv_w, conv_b, batch, seq):
    t, d = hn.shape
    nf = w_a.shape[1]
    tf = FF_TILE
    return pl.pallas_call(
        functools.partial(_glu_up_kernel, row_chunk=256),
        grid=(batch, nf // tf),
        in_specs=[
            pl.BlockSpec((seq, d), lambda b, j: (b, 0)),
            pl.BlockSpec((d, tf), lambda b, j: (0, j)),
            pl.BlockSpec((d, tf), lambda b, j: (0, j)),
            pl.BlockSpec((3, tf), lambda b, j: (0, j)),
            pl.BlockSpec((1, tf), lambda b, j: (0, j)),
        ],
        out_specs=pl.BlockSpec((seq, tf), lambda b, j: (b, j)),
        out_shape=jax.ShapeDtypeStruct((t, nf), BF16),
        scratch_shapes=[pltpu.VMEM((seq + 2 * SUBLANES, tf), F32), pltpu.VMEM((seq, tf), F32)],
        compiler_params=_params(("parallel", "arbitrary")),
        name="glu_up",
    )(hn, w_a, w_v, conv_w, conv_b)


def _proj_res_kernel(x_ref, w_ref, res_ref, out_ref):
    out_ref[...] = res_ref[...] + jnp.dot(x_ref[...], w_ref[...], preferred_element_type=F32)


def proj_res(x, w, res, tm=1024, tn=512):
    t, k = x.shape
    n = w.shape[1]
    return pl.pallas_call(
        _proj_res_kernel,
        grid=(t // tm, n // tn),
        in_specs=[
            pl.BlockSpec((tm, k), lambda i, j: (i, 0)),
            pl.BlockSpec((k, tn), lambda i, j: (0, j)),
            pl.BlockSpec((tm, tn), lambda i, j: (i, j)),
        ],
        out_specs=pl.BlockSpec((tm, tn), lambda i, j: (i, j)),
        out_shape=jax.ShapeDtypeStruct((t, n), F32),
        compiler_params=_params(("parallel", "arbitrary")),
        name="proj_res",
    )(x, w, res)


def conv_glu(h, norm_g, w_up, conv_w, conv_b, w_down, batch, seq):
    fpad = D_FF_PAD - D_FF
    w_a = jnp.pad(w_up[:, :D_FF], ((0, 0), (0, fpad))).astype(BF16)
    w_v = jnp.pad(w_up[:, D_FF:], ((0, 0), (0, fpad))).astype(BF16)
    cw = jnp.pad(conv_w, ((0, 0), (0, fpad)))
    cb = jnp.pad(conv_b, (0, fpad)).reshape(1, D_FF_PAD)
    wd = jnp.pad(w_down, ((0, fpad), (0, 0))).astype(BF16)
    hn = rmsnorm(h, norm_g, BF16)
    g = glu_up(hn, w_a, w_v, cw, cb, batch, seq)
    return proj_res(g, wd, h)


def kernel(x, norm_mix_g, norm_ffn_g, mlstm_w_in, mlstm_b_gate, mlstm_head_g, mlstm_w_out,
           hgrn_w_in, hgrn_lb, hgrn_head_g, hgrn_w_out, ffn_w_up, ffn_conv_w, ffn_conv_b,
           ffn_w_down, final_g):
    batch, seq, d = x.shape
    h = x.reshape(batch * seq, d)
    depth = norm_mix_g.shape[0]
    for layer in range(depth):
        j = layer // 2
        if layer % 2 == 0:
            n_main = 2 * A_HEADS * A_DK + 2 * A_HEADS * A_DV
            w_in = mlstm_w_in[j]
            n_gate = w_in.shape[1] - n_main
            w_gate = jnp.pad(w_in[:, n_main:], ((0, 0), (0, LANES - n_gate))).astype(BF16)
            p, gates = norm_proj(h, norm_mix_g[layer], w_in[:, :n_main].astype(BF16), F32, w_gate=w_gate)
            bias = jnp.pad(mlstm_b_gate[j], (0, LANES - n_gate))
            yf, yb = mlstm_scan(p, gates, bias, batch, seq)
            h = mix_out(yf, yb, p, 2, mlstm_head_g[j], h, mlstm_w_out[j].astype(BF16), A_HEADS, silu=False)
        else:
            p = norm_proj(h, norm_mix_g[layer], hgrn_w_in[j].astype(BF16), F32)
            yf, yb = hgrn_scan(p, hgrn_lb, layer, batch, seq)
            h = mix_out(yf, yb, p, 2, hgrn_head_g[j], h, hgrn_w_out[j].astype(BF16), B_HEADS, silu=True)
        h = conv_glu(h, norm_ffn_g[layer], ffn_w_up[layer], ffn_conv_w[layer], ffn_conv_b[layer],
                     ffn_w_down[layer], batch, seq)
    return rmsnorm(h, final_g, x.dtype).reshape(batch, seq, d)
```

```python
import functools
import math

import numpy as np
import jax
import jax.numpy as jnp
from jax import lax
from jax.experimental import pallas as pl
from jax.experimental.pallas import tpu as pltpu

F32 = jnp.float32
BF16 = jnp.bfloat16

EPS = 1e-6
D_MODEL = 2048
LANES = 128
SUBLANES = 8
MXU_DIM = 256
VMEM_LIMIT_BYTES = 56 * 1024 * 1024

A_HEADS, A_DK, A_DV = 8, 128, 256
A_CHUNK = 256
A_HEADS_PER_STEP = 2
B_HEADS, B_DK, B_DV = 16, 128, 128
B_CHUNK = 128
B_LEVELS = 7
B_HEADS_PER_STEP = 4
D_FF = 5504
FF_TILE = 256
D_FF_PAD = 5632
INV_SQRT2 = 0.7071067811865476
LOG2E = 1.4426950408889634
MASKED = -1e30

_NT = (((1,), (1,)), ((), ()))


def _params(sem):
    return pltpu.CompilerParams(dimension_semantics=sem, vmem_limit_bytes=VMEM_LIMIT_BYTES)


def _rms_rows(x, g):
    ms = jnp.mean(x * x, axis=-1, keepdims=True)
    return x * lax.rsqrt(ms + EPS) * g


def _sigmoid(x):
    return 1.0 / (1.0 + jnp.exp(-x))


def _log_sigmoid(x):
    return jnp.minimum(x, 0.0) - jnp.log1p(jnp.exp(-jnp.abs(x)))


def _norm_kernel(x_ref, g_ref, o_ref):
    o_ref[...] = _rms_rows(x_ref[...], g_ref[...]).astype(o_ref.dtype)


def rmsnorm(x, g, out_dtype, tm=256):
    t, d = x.shape
    return pl.pallas_call(
        _norm_kernel,
        grid=(t // tm,),
        in_specs=[pl.BlockSpec((tm, d), lambda i: (i, 0)), pl.BlockSpec((1, d), lambda i: (0, 0))],
        out_specs=pl.BlockSpec((tm, d), lambda i: (i, 0)),
        out_shape=jax.ShapeDtypeStruct((t, d), out_dtype),
        compiler_params=_params(("parallel",)),
        name="rmsnorm",
    )(x, g.reshape(1, d))


def _norm_proj_kernel(x_ref, g_ref, w_ref, *rest, row_chunk, has_gate):
    if has_gate:
        wg_ref, o_ref, og_ref, hn_ref = rest
    else:
        o_ref, hn_ref = rest

    @pl.when(pl.program_id(1) == 0)
    def _():
        def body(r, carry):
            rows = pl.ds(pl.multiple_of(r * row_chunk, row_chunk), row_chunk)
            hn_ref[rows, :] = _rms_rows(x_ref[rows, :], g_ref[...]).astype(BF16)
            return carry

        lax.fori_loop(0, x_ref.shape[0] // row_chunk, body, 0)
        if has_gate:
            og_ref[...] = jnp.dot(hn_ref[...], wg_ref[...], preferred_element_type=F32)

    o_ref[...] = jnp.dot(hn_ref[...], w_ref[...], preferred_element_type=F32).astype(o_ref.dtype)


def norm_proj(x, g, w, out_dtype, w_gate=None, tm=1024, tn=512):
    t, d = x.shape
    n = w.shape[1]
    has_gate = w_gate is not None
    in_specs = [
        pl.BlockSpec((tm, d), lambda i, j: (i, 0)),
        pl.BlockSpec((1, d), lambda i, j: (0, 0)),
        pl.BlockSpec((d, tn), lambda i, j: (0, j)),
    ]
    out_specs = [pl.BlockSpec((tm, tn), lambda i, j: (i, j))]
    out_shape = [jax.ShapeDtypeStruct((t, n), out_dtype)]
    args = [x, g.reshape(1, d), w]
    if has_gate:
        ng = w_gate.shape[1]
        in_specs.append(pl.BlockSpec((d, ng), lambda i, j: (0, 0)))
        out_specs.append(pl.BlockSpec((tm, ng), lambda i, j: (i, 0)))
        out_shape.append(jax.ShapeDtypeStruct((t, ng), F32))
        args.append(w_gate)
    outs = pl.pallas_call(
        functools.partial(_norm_proj_kernel, row_chunk=64, has_gate=has_gate),
        grid=(t // tm, n // tn),
        in_specs=in_specs,
        out_specs=out_specs,
        out_shape=out_shape,
        scratch_shapes=[pltpu.VMEM((tm, d), BF16)],
        compiler_params=_params(("parallel", "arbitrary")),
        name="norm_proj",
    )(*args)
    return outs if has_gate else outs[0]


class _Stream:
    def __init__(self, **kw):
        self.__dict__.update(kw)


def _sublane_scan(x, op, identity, reverse):
    L = x.shape[0]
    row = lax.broadcasted_iota(jnp.int32, x.shape, 0)
    k = 1
    while k < L:
        if reverse:
            shifted, valid = pltpu.roll(x, L - k, 0), row < L - k
        else:
            shifted, valid = pltpu.roll(x, k, 0), row >= k
        x = op(x, jnp.where(valid, shifted, identity))
        k *= 2
    return x


def _mlstm_gates(ig, f_pre, m_prev, reverse):
    L = ig.shape[0]
    ln_scale = -0.5 * math.log(A_DK)
    last = slice(0, 1) if reverse else slice(L - 1, L)
    b = _sublane_scan(_log_sigmoid(f_pre), jnp.add, 0.0, reverse)
    u = ig - b
    cm = _sublane_scan(u, jnp.maximum, -jnp.inf, reverse)
    mx = jnp.maximum(cm, m_prev)
    mn = jnp.maximum(cm[last, :], m_prev)
    u2 = u * LOG2E
    g = _Stream(
        cv2=(ln_scale - mx) * LOG2E,
        rv2t=jnp.concatenate([u2[i * LANES:(i + 1) * LANES, :].T for i in range(L // LANES)], axis=1),
        w_inter=jnp.exp(m_prev - mx + ln_scale),
        floor=jnp.exp(-mx - b),
        w_s=jnp.exp(u - mn),
        decay=jnp.exp(m_prev - mn),
        m_new=b[last, :] + mn)
    return g


def _mlstm_chunk(streams):
    for s in streams:
        qk = lax.dot_general(s.q, s.k, _NT, preferred_element_type=F32)
        s.p = qk * jnp.exp2(s.cv2 + s.rv2 + s.negmask)
    for s in streams:
        num = (jnp.dot(s.p.astype(BF16), s.v, preferred_element_type=F32)
               + s.w_inter * jnp.dot(s.q, s.ct_ref[...].astype(BF16), preferred_element_type=F32))
        den = (jnp.sum(s.p, axis=-1, keepdims=True)
               + s.w_inter * jnp.sum(s.q.astype(F32) * s.n_ref[...], axis=-1, keepdims=True))
        s.h = num * (1.0 / jnp.maximum(jnp.abs(den), s.floor))
    for s in streams:
        kw = s.k.astype(F32) * s.w_s
        s.ct_ref[...] = s.decay * s.ct_ref[...] + jnp.dot(kw.T.astype(BF16), s.v, preferred_element_type=F32)
        s.n_ref[...] = s.decay * s.n_ref[...] + jnp.sum(kw, axis=0, keepdims=True)


def _mlstm_kernel(bi_ref, bf_ref, qf_ref, kf_ref, vf_ref, gif_ref, gff_ref, qb_ref, kb_ref, vb_ref, gib_ref,
                  gfb_ref, negf_ref, negb_ref, yf_ref, yb_ref, ctf, nf, mf, ctb, nb, mb, *, heads):
    @pl.when(pl.program_id(2) == 0)
    def _():
        for ref in (ctf, nf, mf, ctb, nb, mb):
            ref[...] = jnp.zeros_like(ref)

    gates_f = _mlstm_gates(gif_ref[0, 0] + bi_ref[0], gff_ref[0, 0] + bf_ref[0], mf[...], False)
    gates_b = _mlstm_gates(gib_ref[0, 0] + bi_ref[0], gfb_ref[0, 0] + bf_ref[0], mb[...], True)
    mf[...] = gates_f.m_new
    mb[...] = gates_b.m_new

    streams = []
    for hh in range(heads):
        ks = slice(hh * A_DK, (hh + 1) * A_DK)
        vs = slice(hh * A_DV, (hh + 1) * A_DV)
        for g, lane, q_ref, k_ref, v_ref, neg_ref, ct, n in (
                (gates_f, hh, qf_ref, kf_ref, vf_ref, negf_ref, ctf, nf),
                (gates_b, heads + hh, qb_ref, kb_ref, vb_ref, negb_ref, ctb, nb)):
            col = slice(lane, lane + 1)
            streams.append(_Stream(
                q=q_ref[:, ks].astype(BF16), k=k_ref[:, ks].astype(BF16), v=v_ref[:, vs].astype(BF16),
                cv2=g.cv2[:, col], rv2=g.rv2t[col, :], w_inter=g.w_inter[:, col], floor=g.floor[:, col],
                w_s=g.w_s[:, col], decay=g.decay[:, col], negmask=neg_ref[...],
                ct_ref=ct.at[hh], n_ref=n.at[hh]))
    _mlstm_chunk(streams)
    for hh in range(heads):
        vs = slice(hh * A_DV, (hh + 1) * A_DV)
        yf_ref[:, vs] = streams[2 * hh].h
        yb_ref[:, vs] = streams[2 * hh + 1].h


def _group_gate_lanes(x, batch, seq, heads):
    ng = A_HEADS // heads
    x = x[:, :4 * A_HEADS].reshape(batch, seq, 2, 2, ng, heads)
    x = jnp.transpose(x, (3, 0, 4, 1, 2, 5)).reshape(2, batch, ng, seq, 2 * heads)
    x = jnp.pad(x, ((0, 0),) * 4 + ((0, LANES - 2 * heads),))
    return x[0], x[1]


def mlstm_scan(p, gates, bias, batch, seq, heads=A_HEADS_PER_STEP):
    L = A_CHUNK
    nc = seq // L
    t = batch * seq
    ng = A_HEADS // heads
    g_in, g_forget = _group_gate_lanes(gates, batch, seq, heads)
    b_in, b_forget = _group_gate_lanes(bias.reshape(1, -1), 1, 1, heads)
    keep = np.tril(np.ones((L, L), np.float32))
    neg_f = jnp.asarray((1.0 - keep) * MASKED)
    neg_b = jnp.asarray((1.0 - keep.T) * MASKED)

    fwd = lambda b, h, c: b * nc + c
    bwd = lambda b, h, c: b * nc + (nc - 1 - c)

    def dir_specs(row):
        gate = pl.BlockSpec((1, 1, L, LANES), lambda b, h, c: (b, h, row(b, h, c) - b * nc, 0))
        return [
            pl.BlockSpec((L, heads * A_DK), lambda b, h, c: (row(b, h, c), h)),
            pl.BlockSpec((L, heads * A_DK), lambda b, h, c: (row(b, h, c), ng + h)),
            pl.BlockSpec((L, heads * A_DV), lambda b, h, c: (row(b, h, c), ng + h)),
            gate, gate,
        ]

    const = pl.BlockSpec((L, L), lambda b, h, c: (0, 0))
    bias_spec = pl.BlockSpec((1, 1, LANES), lambda b, h, c: (h, 0, 0))
    in_specs = [bias_spec, bias_spec] + dir_specs(fwd) + dir_specs(bwd) + [const, const]
    out_specs = [
        pl.BlockSpec((L, heads * A_DV), lambda b, h, c: (fwd(b, h, c), h)),
        pl.BlockSpec((L, heads * A_DV), lambda b, h, c: (bwd(b, h, c), h)),
    ]
    state = [pltpu.VMEM((heads, A_DK, A_DV), F32), pltpu.VMEM((heads, 1, A_DK), F32),
             pltpu.VMEM((1, LANES), F32)]
    return pl.pallas_call(
        functools.partial(_mlstm_kernel, heads=heads),
        grid=(batch, ng, nc),
        in_specs=in_specs,
        out_specs=out_specs,
        out_shape=[jax.ShapeDtypeStruct((t, A_HEADS * A_DV), F32)] * 2,
        scratch_shapes=state + state,
        compiler_params=_params(("parallel", "parallel", "arbitrary")),
        name="mlstm_scan",
    )(b_in[0], b_forget[0], p, p, p, g_in, g_forget, p, p, p, g_in, g_forget, neg_f, neg_b)


def _mid_rows(b, c, r_off):
    L = b.shape[0]
    if c >= SUBLANES:
        b3 = b.reshape(L // c, c, LANES)
        return jnp.broadcast_to(b3[:, r_off:r_off + 1, :], b3.shape).reshape(L, LANES)
    b3 = b.reshape(L // SUBLANES, SUBLANES, LANES)
    lo = jnp.broadcast_to(b3[:, r_off:r_off + 1, :], b3.shape)
    hi = jnp.broadcast_to(b3[:, c + r_off:c + r_off + 1, :], b3.shape)
    sub = lax.broadcasted_iota(jnp.int32, b3.shape, 1)
    return jnp.where(sub < c, lo, hi).reshape(L, LANES)


def _split3(x):
    hi = x.astype(BF16)
    r1 = x - hi.astype(F32)
    mid = r1.astype(BF16)
    lo = (r1 - mid.astype(F32)).astype(BF16)
    return hi, mid, lo


def _hgrn_chunk(streams):
    L = streams[0].q.shape[0]
    row = lax.broadcasted_iota(jnp.int32, (L, LANES), 0)
    odd = (row & 1) == 1

    for s in streams:
        e = jnp.exp(-jnp.abs(s.f))
        r = 1.0 / (1.0 + e)
        er = e * r
        pos = s.f >= 0.0
        s.forget = s.lb + (1.0 - s.lb) * jnp.where(pos, r, er)
        s.kk = (1.0 - s.lb) * jnp.where(pos, er, r)
    for s in streams:
        s.b = jnp.dot(s.tri3, jnp.concatenate(_split3(jnp.log2(s.forget)), axis=0),
                      preferred_element_type=F32)
        s.b_tot = s.b[0:1, :] if s.reverse else s.b[L - 1:L, :]
    for s in streams:
        s.st = s.st_ref[...]
        s.o = lax.dot_general((s.q * jnp.exp2(s.b)).astype(BF16), s.st.astype(BF16), _NT,
                              preferred_element_type=F32)
        s.attn = None

    for j in range(B_LEVELS):
        half = 1 << j
        for s in streams:
            if j == 0:
                w = jnp.where(odd != s.reverse, s.forget, 1.0)
            else:
                b_mid = _mid_rows(s.b, 2 * half, half if s.reverse else half - 1)
                w = jnp.exp2(-jnp.abs(s.b - b_mid))
            pj = lax.dot_general((s.q * w).astype(BF16), (s.kk * w).astype(BF16), _NT,
                                 preferred_element_type=F32)
            pj = pj * s.masks_ref[j]
            s.attn = pj if s.attn is None else s.attn + pj
    for s in streams:
        s.o = s.o + jnp.dot(s.attn.astype(BF16), s.v.astype(BF16), preferred_element_type=F32)
        s.o = s.o + jnp.sum(s.q * s.kk, axis=-1, keepdims=True) * s.v
    for s in streams:
        kt = (s.kk * jnp.exp2(s.b_tot - s.b)).astype(BF16)
        s.st_ref[...] = jnp.exp2(s.b_tot) * s.st + jnp.dot(s.v.T.astype(BF16), kt, preferred_element_type=F32)


def _hgrn_kernel(lbp_ref, qf_ref, vf_ref, ff_ref, qb_ref, vb_ref, fb_ref,
                 trif_ref, trib_ref, maskf_ref, maskb_ref, yf_ref, yb_ref, stf, stb, *, layer, heads):
    @pl.when(pl.program_id(2) == 0)
    def _():
        stf[...] = jnp.zeros_like(stf)
        stb[...] = jnp.zeros_like(stb)

    lbp = lbp_ref[...]
    ex = jnp.exp(lbp - jnp.max(lbp, axis=0, keepdims=True))
    sm = ex / jnp.sum(ex, axis=0, keepdims=True)
    lb = jnp.sum(sm[0:layer + 1, :], axis=0, keepdims=True) - sm[0:1, :]

    tri_f, tri_b = trif_ref[...], trib_ref[...]
    streams = []
    for hh in range(heads):
        ks = slice(hh * B_DK, (hh + 1) * B_DK)
        vs = slice(hh * B_DV, (hh + 1) * B_DV)
        streams.append(_Stream(q=qf_ref[:, ks].astype(F32), v=vf_ref[:, vs].astype(F32),
                               f=ff_ref[:, ks].astype(F32), lb=lb[:, ks], tri3=tri_f,
                               masks_ref=maskf_ref, st_ref=stf.at[hh], reverse=False))
        streams.append(_Stream(q=qb_ref[:, ks].astype(F32), v=vb_ref[:, vs].astype(F32),
                               f=fb_ref[:, ks].astype(F32), lb=lb[:, ks], tri3=tri_b,
                               masks_ref=maskb_ref, st_ref=stb.at[hh], reverse=True))
    _hgrn_chunk(streams)
    for hh in range(heads):
        vs = slice(hh * B_DV, (hh + 1) * B_DV)
        yf_ref[:, vs] = streams[2 * hh].o
        yb_ref[:, vs] = streams[2 * hh + 1].o


def _hgrn_masks(L):
    t = np.arange(L)[:, None]
    s = np.arange(L)[None, :]
    out = []
    for j in range(B_LEVELS):
        half = 1 << j
        c = 2 * half
        out.append(((t // c == s // c) & (t % c >= half) & (s % c < half)).astype(np.float32))
    return np.stack(out)


def hgrn_scan(p, lb_param, layer, batch, seq, heads=B_HEADS_PER_STEP):
    L = B_CHUNK
    nc = seq // L
    t = batch * seq
    ng = B_HEADS // heads
    tri_f = jnp.asarray(np.tile(np.tril(np.ones((L, L), np.float32)), (1, 3)), dtype=BF16)
    tri_b = jnp.asarray(np.tile(np.triu(np.ones((L, L), np.float32)), (1, 3)), dtype=BF16)
    masks_f = _hgrn_masks(L)
    masks_b = np.ascontiguousarray(np.transpose(masks_f, (0, 2, 1)))
    depth = lb_param.shape[0]

    fwd = lambda b, h, c: b * nc + c
    bwd = lambda b, h, c: b * nc + (nc - 1 - c)

    def dir_specs(row, fcol):
        return [
            pl.BlockSpec((L, heads * B_DK), lambda b, h, c: (row(b, h, c), h)),
            pl.BlockSpec((L, heads * B_DV), lambda b, h, c: (row(b, h, c), ng + h)),
            pl.BlockSpec((L, heads * B_DK), lambda b, h, c: (row(b, h, c), fcol * ng + h)),
        ]

    const2 = pl.BlockSpec((L, 3 * L), lambda b, h, c: (0, 0))
    const3 = pl.BlockSpec((B_LEVELS, L, L), lambda b, h, c: (0, 0, 0))
    in_specs = ([pl.BlockSpec((depth, heads * B_DK), lambda b, h, c: (0, h))]
                + dir_specs(fwd, 3) + dir_specs(bwd, 4) + [const2, const2, const3, const3])
    out_specs = [
        pl.BlockSpec((L, heads * B_DV), lambda b, h, c: (fwd(b, h, c), h)),
        pl.BlockSpec((L, heads * B_DV), lambda b, h, c: (bwd(b, h, c), h)),
    ]
    return pl.pallas_call(
        functools.partial(_hgrn_kernel, layer=layer, heads=heads),
        grid=(batch, ng, nc),
        in_specs=in_specs,
        out_specs=out_specs,
        out_shape=[jax.ShapeDtypeStruct((t, B_HEADS * B_DV), F32)] * 2,
        scratch_shapes=[pltpu.VMEM((heads, B_DV, B_DK), F32)] * 2,
        compiler_params=_params(("parallel", "parallel", "arbitrary")),
        name="hgrn_scan",
    )(lb_param, p, p, p, p, p, p, tri_f, tri_b, jnp.asarray(masks_f), jnp.asarray(masks_b))


def _mix_out_kernel(yf_ref, yb_ref, gate_ref, hg_ref, res_ref, w_ref, ng_ref, out_ref, hn_ref, *, d_head, silu):
    d = res_ref.shape[1]
    acc = res_ref[...]
    for c0 in range(0, d, MXU_DIM):
        parts = []
        for h0 in range(c0, c0 + MXU_DIM, d_head):
            sl = slice(h0, h0 + d_head)
            y = yf_ref[:, sl] + yb_ref[:, sl]
            ms = jnp.mean(y * y, axis=-1, keepdims=True)
            gate = gate_ref[:, sl].astype(F32)
            act = _sigmoid(gate)
            if silu:
                act = gate * act
            parts.append((y * lax.rsqrt(ms + EPS) * hg_ref[:, sl] * act).astype(BF16))
        z = parts[0] if len(parts) == 1 else jnp.concatenate(parts, axis=1)
        acc = acc + jnp.dot(z, w_ref[c0:c0 + MXU_DIM, :], preferred_element_type=F32)
    out_ref[...] = acc
    hn_ref[...] = _rms_rows(acc, ng_ref[...]).astype(hn_ref.dtype)


def mix_out(yf, yb, p, gate_col_block, head_g, res, w_out, norm_g, d_head, silu, tm=256):
    t, d = yf.shape
    row = lambda i: (i, 0)
    const = lambda i: (0, 0)
    return pl.pallas_call(
        functools.partial(_mix_out_kernel, d_head=d_head, silu=silu),
        grid=(t // tm,),
        in_specs=[
            pl.BlockSpec((tm, d), row),
            pl.BlockSpec((tm, d), row),
            pl.BlockSpec((tm, d), lambda i: (i, gate_col_block)),
            pl.BlockSpec((1, d), const),
            pl.BlockSpec((tm, d), row),
            pl.BlockSpec((d, d), const),
            pl.BlockSpec((1, d), const),
        ],
        out_specs=[pl.BlockSpec((tm, d), row), pl.BlockSpec((tm, d), row)],
        out_shape=[jax.ShapeDtypeStruct((t, d), F32), jax.ShapeDtypeStruct((t, d), BF16)],
        compiler_params=_params(("parallel",)),
        name="mix_out",
    )(yf, yb, p, head_g.reshape(1, d), res, w_out, norm_g.reshape(1, d))


def _glu_up_kernel(hn_ref, w_ref, cw_ref, cb_ref, g_ref, a_scr, v_scr, *, row_chunk):
    s = hn_ref.shape[0]
    tf = a_scr.shape[1]
    pad = SUBLANES
    zeros = jnp.zeros((pad, tf), F32)
    a_scr[0:pad, :] = zeros
    a_scr[pad + s:2 * pad + s, :] = zeros

    def epilogue(first, last):
        n = last - first
        ext = a_scr[first:first + n + 2 * pad, :]
        prev = pltpu.roll(ext, 1, 0)[pad:pad + n, :]
        nxt = pltpu.roll(ext, n + 2 * pad - 1, 0)[pad:pad + n, :]
        c = (cb_ref[...] + prev * cw_ref[0:1, :] + ext[pad:pad + n, :] * cw_ref[1:2, :]
             + nxt * cw_ref[2:3, :])
        gelu = 0.5 * c * (1.0 + lax.erf(c * INV_SQRT2))
        g_ref[first:last, :] = (gelu * v_scr[first:last, :]).astype(BF16)

    trail = 2 * SUBLANES
    bounds = [0]
    while s - bounds[-1] > row_chunk:
        bounds.append(bounds[-1] + (s - bounds[-1]) // 2)
    bounds.append(s)
    done = 0
    for lo, hi in zip(bounds[:-1], bounds[1:]):
        if lo > 0:
            epilogue(done, lo - trail)
            done = lo - trail
        u = jnp.dot(hn_ref[lo:hi, :], w_ref[...], preferred_element_type=F32)
        a_scr[pad + lo:pad + hi, :] = u[:, :tf]
        v_scr[lo:hi, :] = u[:, tf:]
    epilogue(done, s)


def glu_up(hn, w_av, conv_w, conv_b, batch, seq):
    t, d = hn.shape
    nf = w_av.shape[1] // 2
    tf = FF_TILE
    return pl.pallas_call(
        functools.partial(_glu_up_kernel, row_chunk=256),
        grid=(batch, nf // tf),
        in_specs=[
            pl.BlockSpec((seq, d), lambda b, j: (b, 0)),
            pl.BlockSpec((d, 2 * tf), lambda b, j: (0, j)),
            pl.BlockSpec((3, tf), lambda b, j: (0, j)),
            pl.BlockSpec((1, tf), lambda b, j: (0, j)),
        ],
        out_specs=pl.BlockSpec((seq, tf), lambda b, j: (b, j)),
        out_shape=jax.ShapeDtypeStruct((t, nf), BF16),
        scratch_shapes=[pltpu.VMEM((seq + 2 * SUBLANES, tf), F32), pltpu.VMEM((seq, tf), F32)],
        compiler_params=_params(("parallel", "arbitrary")),
        name="glu_up",
    )(hn, w_av, conv_w, conv_b)


def _proj_res_kernel(x_ref, w_ref, res_ref, out_ref):
    out_ref[...] = res_ref[...] + jnp.dot(x_ref[...], w_ref[...], preferred_element_type=F32)


def proj_res(x, w, res, tm=1024, tn=512):
    t, k = x.shape
    n = w.shape[1]
    return pl.pallas_call(
        _proj_res_kernel,
        grid=(t // tm, n // tn),
        in_specs=[
            pl.BlockSpec((tm, k), lambda i, j: (i, 0)),
            pl.BlockSpec((k, tn), lambda i, j: (0, j)),
            pl.BlockSpec((tm, tn), lambda i, j: (i, j)),
        ],
        out_specs=pl.BlockSpec((tm, tn), lambda i, j: (i, j)),
        out_shape=jax.ShapeDtypeStruct((t, n), F32),
        compiler_params=_params(("parallel", "arbitrary")),
        name="proj_res",
    )(x, w, res)


def conv_glu(h, hn, w_up, conv_w, conv_b, w_down, batch, seq):
    fpad = D_FF_PAD - D_FF
    nj = D_FF_PAD // FF_TILE
    w_a = jnp.pad(w_up[:, :D_FF], ((0, 0), (0, fpad))).astype(BF16).reshape(-1, nj, 1, FF_TILE)
    w_v = jnp.pad(w_up[:, D_FF:], ((0, 0), (0, fpad))).astype(BF16).reshape(-1, nj, 1, FF_TILE)
    w_av = jnp.concatenate([w_a, w_v], axis=2).reshape(-1, 2 * D_FF_PAD)
    cw = jnp.pad(conv_w, ((0, 0), (0, fpad)))
    cb = jnp.pad(conv_b, (0, fpad)).reshape(1, D_FF_PAD)
    wd = jnp.pad(w_down, ((0, fpad), (0, 0))).astype(BF16)
    g = glu_up(hn, w_av, cw, cb, batch, seq)
    return proj_res(g, wd, h)


def kernel(x, norm_mix_g, norm_ffn_g, mlstm_w_in, mlstm_b_gate, mlstm_head_g, mlstm_w_out,
           hgrn_w_in, hgrn_lb, hgrn_head_g, hgrn_w_out, ffn_w_up, ffn_conv_w, ffn_conv_b,
           ffn_w_down, final_g):
    batch, seq, d = x.shape
    h = x.reshape(batch * seq, d)
    depth = norm_mix_g.shape[0]
    for layer in range(depth):
        j = layer // 2
        if layer % 2 == 0:
            n_main = 2 * A_HEADS * A_DK + 2 * A_HEADS * A_DV
            w_in = mlstm_w_in[j]
            n_gate = w_in.shape[1] - n_main
            w_gate = jnp.pad(w_in[:, n_main:], ((0, 0), (0, LANES - n_gate))).astype(BF16)
            p, gates = norm_proj(h, norm_mix_g[layer], w_in[:, :n_main].astype(BF16), F32, w_gate=w_gate)
            bias = jnp.pad(mlstm_b_gate[j], (0, LANES - n_gate))
            yf, yb = mlstm_scan(p, gates, bias, batch, seq)
            h, hn = mix_out(yf, yb, p, 2, mlstm_head_g[j], h, mlstm_w_out[j].astype(BF16),
                            norm_ffn_g[layer], A_DV, silu=False)
        else:
            p = norm_proj(h, norm_mix_g[layer], hgrn_w_in[j].astype(BF16), F32)
            yf, yb = hgrn_scan(p, hgrn_lb, layer, batch, seq)
            h, hn = mix_out(yf, yb, p, 2, hgrn_head_g[j], h, hgrn_w_out[j].astype(BF16),
                            norm_ffn_g[layer], B_DV, silu=True)
        h = conv_glu(h, hn, ffn_w_up[layer], ffn_conv_w[layer], ffn_conv_b[layer], ffn_w_down[layer],
                     batch, seq)
    return rmsnorm(h, final_g, x.dtype).reshape(batch, seq, d)
```

```python
import functools
import math

import numpy as np
import jax
import jax.numpy as jnp
from jax import lax
from jax.experimental import pallas as pl
from jax.experimental.pallas import tpu as pltpu

F32 = jnp.float32
BF16 = jnp.bfloat16

EPS = 1e-6
D_MODEL = 2048
LANES = 128
SUBLANES = 8
MXU_DIM = 256
VMEM_LIMIT_BYTES = 56 * 1024 * 1024

A_HEADS, A_DK, A_DV = 8, 128, 256
A_CHUNK = 256
A_HEADS_PER_STEP = 2
B_HEADS, B_DK, B_DV = 16, 128, 128
B_CHUNK = 128
B_LEVELS = 7
B_HEADS_PER_STEP = 4
D_FF = 5504
FF_TILE = 256
D_FF_PAD = 5632
INV_SQRT2 = 0.7071067811865476
LOG2E = 1.4426950408889634
MASKED = -1e30

_NT = (((1,), (1,)), ((), ()))


def _params(sem):
    return pltpu.CompilerParams(dimension_semantics=sem, vmem_limit_bytes=VMEM_LIMIT_BYTES)


def _rms_rows(x, g):
    ms = jnp.mean(x * x, axis=-1, keepdims=True)
    return x * lax.rsqrt(ms + EPS) * g


def _sigmoid(x):
    return 1.0 / (1.0 + jnp.exp(-x))


def _log_sigmoid(x):
    return jnp.minimum(x, 0.0) - jnp.log1p(jnp.exp(-jnp.abs(x)))


def _norm_kernel(x_ref, g_ref, o_ref):
    o_ref[...] = _rms_rows(x_ref[...], g_ref[...]).astype(o_ref.dtype)


def rmsnorm(x, g, out_dtype, tm=256):
    t, d = x.shape
    return pl.pallas_call(
        _norm_kernel,
        grid=(t // tm,),
        in_specs=[pl.BlockSpec((tm, d), lambda i: (i, 0)), pl.BlockSpec((1, d), lambda i: (0, 0))],
        out_specs=pl.BlockSpec((tm, d), lambda i: (i, 0)),
        out_shape=jax.ShapeDtypeStruct((t, d), out_dtype),
        compiler_params=_params(("parallel",)),
        name="rmsnorm",
    )(x, g.reshape(1, d))


def _norm_proj_kernel(x_ref, g_ref, w_ref, *rest, row_chunk, has_gate):
    if has_gate:
        wg_ref, o_ref, og_ref, hn_ref = rest
    else:
        o_ref, hn_ref = rest

    @pl.when(pl.program_id(1) == 0)
    def _():
        def body(r, carry):
            rows = pl.ds(pl.multiple_of(r * row_chunk, row_chunk), row_chunk)
            hn_ref[rows, :] = _rms_rows(x_ref[rows, :], g_ref[...]).astype(BF16)
            return carry

        lax.fori_loop(0, x_ref.shape[0] // row_chunk, body, 0)
        if has_gate:
            og_ref[...] = jnp.dot(hn_ref[...], wg_ref[...], preferred_element_type=F32)

    o_ref[...] = jnp.dot(hn_ref[...], w_ref[0].astype(BF16), preferred_element_type=F32).astype(o_ref.dtype)


def norm_proj(x, g, w, layer, n, out_dtype, w_gate=None, tm=1024, tn=512):
    t, d = x.shape
    has_gate = w_gate is not None
    in_specs = [
        pl.BlockSpec((tm, d), lambda i, j: (i, 0)),
        pl.BlockSpec((1, d), lambda i, j: (0, 0)),
        pl.BlockSpec((1, d, tn), lambda i, j: (layer, 0, j)),
    ]
    out_specs = [pl.BlockSpec((tm, tn), lambda i, j: (i, j))]
    out_shape = [jax.ShapeDtypeStruct((t, n), out_dtype)]
    args = [x, g.reshape(1, d), w]
    if has_gate:
        ng = w_gate.shape[1]
        in_specs.append(pl.BlockSpec((d, ng), lambda i, j: (0, 0)))
        out_specs.append(pl.BlockSpec((tm, ng), lambda i, j: (i, 0)))
        out_shape.append(jax.ShapeDtypeStruct((t, ng), F32))
        args.append(w_gate)
    outs = pl.pallas_call(
        functools.partial(_norm_proj_kernel, row_chunk=64, has_gate=has_gate),
        grid=(t // tm, n // tn),
        in_specs=in_specs,
        out_specs=out_specs,
        out_shape=out_shape,
        scratch_shapes=[pltpu.VMEM((tm, d), BF16)],
        compiler_params=_params(("parallel", "arbitrary")),
        name="norm_proj",
    )(*args)
    return outs if has_gate else outs[0]


class _Stream:
    def __init__(self, **kw):
        self.__dict__.update(kw)


def _sublane_scan(x, op, identity, reverse):
    L = x.shape[0]
    row = lax.broadcasted_iota(jnp.int32, x.shape, 0)
    k = 1
    while k < L:
        if reverse:
            shifted, valid = pltpu.roll(x, L - k, 0), row < L - k
        else:
            shifted, valid = pltpu.roll(x, k, 0), row >= k
        x = op(x, jnp.where(valid, shifted, identity))
        k *= 2
    return x


def _mlstm_gates(ig, f_pre, m_prev, reverse):
    L = ig.shape[0]
    ln_scale = -0.5 * math.log(A_DK)
    last = slice(0, 1) if reverse else slice(L - 1, L)
    b = _sublane_scan(_log_sigmoid(f_pre), jnp.add, 0.0, reverse)
    u = ig - b
    cm = _sublane_scan(u, jnp.maximum, -jnp.inf, reverse)
    mx = jnp.maximum(cm, m_prev)
    mn = jnp.maximum(cm[last, :], m_prev)
    u2 = u * LOG2E
    g = _Stream(
        cv2=(ln_scale - mx) * LOG2E,
        rv2t=jnp.concatenate([u2[i * LANES:(i + 1) * LANES, :].T for i in range(L // LANES)], axis=1),
        w_inter=jnp.exp(m_prev - mx + ln_scale),
        floor=jnp.exp(-mx - b),
        w_s=jnp.exp(u - mn),
        decay=jnp.exp(m_prev - mn),
        m_new=b[last, :] + mn)
    return g


def _mlstm_chunk(streams):
    for s in streams:
        qk = lax.dot_general(s.q, s.k, _NT, preferred_element_type=F32)
        s.p = qk * jnp.exp2(s.cv2 + s.rv2 + s.negmask)
    for s in streams:
        num = (jnp.dot(s.p.astype(BF16), s.v, preferred_element_type=F32)
               + s.w_inter * jnp.dot(s.q, s.ct_ref[...].astype(BF16), preferred_element_type=F32))
        den = (jnp.sum(s.p, axis=-1, keepdims=True)
               + s.w_inter * jnp.sum(s.q.astype(F32) * s.n_ref[...], axis=-1, keepdims=True))
        s.h = num * (1.0 / jnp.maximum(jnp.abs(den), s.floor))
    for s in streams:
        kw = s.k.astype(F32) * s.w_s
        s.ct_ref[...] = s.decay * s.ct_ref[...] + jnp.dot(kw.T.astype(BF16), s.v, preferred_element_type=F32)
        s.n_ref[...] = s.decay * s.n_ref[...] + jnp.sum(kw, axis=0, keepdims=True)


def _mlstm_kernel(bi_ref, bf_ref, qf_ref, kf_ref, vf_ref, gif_ref, gff_ref, qb_ref, kb_ref, vb_ref, gib_ref,
                  gfb_ref, negf_ref, negb_ref, yf_ref, yb_ref, ctf, nf, mf, ctb, nb, mb, *, heads):
    @pl.when(pl.program_id(2) == 0)
    def _():
        for ref in (ctf, nf, mf, ctb, nb, mb):
            ref[...] = jnp.zeros_like(ref)

    gates_f = _mlstm_gates(gif_ref[0, 0] + bi_ref[0], gff_ref[0, 0] + bf_ref[0], mf[...], False)
    gates_b = _mlstm_gates(gib_ref[0, 0] + bi_ref[0], gfb_ref[0, 0] + bf_ref[0], mb[...], True)
    mf[...] = gates_f.m_new
    mb[...] = gates_b.m_new

    streams = []
    for hh in range(heads):
        ks = slice(hh * A_DK, (hh + 1) * A_DK)
        vs = slice(hh * A_DV, (hh + 1) * A_DV)
        for g, lane, q_ref, k_ref, v_ref, neg_ref, ct, n in (
                (gates_f, hh, qf_ref, kf_ref, vf_ref, negf_ref, ctf, nf),
                (gates_b, heads + hh, qb_ref, kb_ref, vb_ref, negb_ref, ctb, nb)):
            col = slice(lane, lane + 1)
            streams.append(_Stream(
                q=q_ref[:, ks].astype(BF16), k=k_ref[:, ks].astype(BF16), v=v_ref[:, vs].astype(BF16),
                cv2=g.cv2[:, col], rv2=g.rv2t[col, :], w_inter=g.w_inter[:, col], floor=g.floor[:, col],
                w_s=g.w_s[:, col], decay=g.decay[:, col], negmask=neg_ref[...],
                ct_ref=ct.at[hh], n_ref=n.at[hh]))
    _mlstm_chunk(streams)
    for hh in range(heads):
        vs = slice(hh * A_DV, (hh + 1) * A_DV)
        yf_ref[:, vs] = streams[2 * hh].h
        yb_ref[:, vs] = streams[2 * hh + 1].h


def _group_gate_lanes(x, batch, seq, heads):
    ng = A_HEADS // heads
    x = x[:, :4 * A_HEADS].reshape(batch, seq, 2, 2, ng, heads)
    x = jnp.transpose(x, (3, 0, 4, 1, 2, 5)).reshape(2, batch, ng, seq, 2 * heads)
    x = jnp.pad(x, ((0, 0),) * 4 + ((0, LANES - 2 * heads),))
    return x[0], x[1]


def mlstm_scan(p, gates, bias, batch, seq, heads=A_HEADS_PER_STEP):
    L = A_CHUNK
    nc = seq // L
    t = batch * seq
    ng = A_HEADS // heads
    g_in, g_forget = _group_gate_lanes(gates, batch, seq, heads)
    b_in, b_forget = _group_gate_lanes(bias.reshape(1, -1), 1, 1, heads)
    keep = np.tril(np.ones((L, L), np.float32))
    neg_f = jnp.asarray((1.0 - keep) * MASKED)
    neg_b = jnp.asarray((1.0 - keep.T) * MASKED)

    fwd = lambda b, h, c: b * nc + c
    bwd = lambda b, h, c: b * nc + (nc - 1 - c)

    def dir_specs(row):
        gate = pl.BlockSpec((1, 1, L, LANES), lambda b, h, c: (b, h, row(b, h, c) - b * nc, 0))
        return [
            pl.BlockSpec((L, heads * A_DK), lambda b, h, c: (row(b, h, c), h)),
            pl.BlockSpec((L, heads * A_DK), lambda b, h, c: (row(b, h, c), ng + h)),
            pl.BlockSpec((L, heads * A_DV), lambda b, h, c: (row(b, h, c), ng + h)),
            gate, gate,
        ]

    const = pl.BlockSpec((L, L), lambda b, h, c: (0, 0))
    bias_spec = pl.BlockSpec((1, 1, LANES), lambda b, h, c: (h, 0, 0))
    in_specs = [bias_spec, bias_spec] + dir_specs(fwd) + dir_specs(bwd) + [const, const]
    out_specs = [
        pl.BlockSpec((L, heads * A_DV), lambda b, h, c: (fwd(b, h, c), h)),
        pl.BlockSpec((L, heads * A_DV), lambda b, h, c: (bwd(b, h, c), h)),
    ]
    state = [pltpu.VMEM((heads, A_DK, A_DV), F32), pltpu.VMEM((heads, 1, A_DK), F32),
             pltpu.VMEM((1, LANES), F32)]
    return pl.pallas_call(
        functools.partial(_mlstm_kernel, heads=heads),
        grid=(batch, ng, nc),
        in_specs=in_specs,
        out_specs=out_specs,
        out_shape=[jax.ShapeDtypeStruct((t, A_HEADS * A_DV), F32)] * 2,
        scratch_shapes=state + state,
        compiler_params=_params(("parallel", "parallel", "arbitrary")),
        name="mlstm_scan",
    )(b_in[0], b_forget[0], p, p, p, g_in, g_forget, p, p, p, g_in, g_forget, neg_f, neg_b)


def _mid_rows(b, c, r_off):
    L = b.shape[0]
    if c >= SUBLANES:
        b3 = b.reshape(L // c, c, LANES)
        return jnp.broadcast_to(b3[:, r_off:r_off + 1, :], b3.shape).reshape(L, LANES)
    b3 = b.reshape(L // SUBLANES, SUBLANES, LANES)
    lo = jnp.broadcast_to(b3[:, r_off:r_off + 1, :], b3.shape)
    hi = jnp.broadcast_to(b3[:, c + r_off:c + r_off + 1, :], b3.shape)
    sub = lax.broadcasted_iota(jnp.int32, b3.shape, 1)
    return jnp.where(sub < c, lo, hi).reshape(L, LANES)


def _split3(x):
    hi = x.astype(BF16)
    r1 = x - hi.astype(F32)
    mid = r1.astype(BF16)
    lo = (r1 - mid.astype(F32)).astype(BF16)
    return hi, mid, lo


def _hgrn_chunk(streams):
    L = streams[0].q.shape[0]
    row = lax.broadcasted_iota(jnp.int32, (L, LANES), 0)
    odd = (row & 1) == 1

    for s in streams:
        e = jnp.exp(-jnp.abs(s.f))
        r = 1.0 / (1.0 + e)
        er = e * r
        pos = s.f >= 0.0
        s.forget = s.lb + (1.0 - s.lb) * jnp.where(pos, r, er)
        s.kk = (1.0 - s.lb) * jnp.where(pos, er, r)
    for s in streams:
        s.b = jnp.dot(s.tri3, jnp.concatenate(_split3(jnp.log2(s.forget)), axis=0),
                      preferred_element_type=F32)
        s.b_tot = s.b[0:1, :] if s.reverse else s.b[L - 1:L, :]
    for s in streams:
        s.st = s.st_ref[...]
        s.o = lax.dot_general((s.q * jnp.exp2(s.b)).astype(BF16), s.st.astype(BF16), _NT,
                              preferred_element_type=F32)
        s.attn = None

    for j in range(B_LEVELS):
        half = 1 << j
        for s in streams:
            if j == 0:
                w = jnp.where(odd != s.reverse, s.forget, 1.0)
            else:
                b_mid = _mid_rows(s.b, 2 * half, half if s.reverse else half - 1)
                w = jnp.exp2(-jnp.abs(s.b - b_mid))
            pj = lax.dot_general((s.q * w).astype(BF16), (s.kk * w).astype(BF16), _NT,
                                 preferred_element_type=F32)
            pj = pj * s.masks_ref[j]
            s.attn = pj if s.attn is None else s.attn + pj
    for s in streams:
        s.o = s.o + jnp.dot(s.attn.astype(BF16), s.v.astype(BF16), preferred_element_type=F32)
        s.o = s.o + jnp.sum(s.q * s.kk, axis=-1, keepdims=True) * s.v
    for s in streams:
        kt = (s.kk * jnp.exp2(s.b_tot - s.b)).astype(BF16)
        s.st_ref[...] = jnp.exp2(s.b_tot) * s.st + jnp.dot(s.v.T.astype(BF16), kt, preferred_element_type=F32)


def _hgrn_kernel(lbp_ref, qf_ref, vf_ref, ff_ref, qb_ref, vb_ref, fb_ref,
                 trif_ref, trib_ref, maskf_ref, maskb_ref, yf_ref, yb_ref, stf, stb, *, layer, heads):
    @pl.when(pl.program_id(2) == 0)
    def _():
        stf[...] = jnp.zeros_like(stf)
        stb[...] = jnp.zeros_like(stb)

    lbp = lbp_ref[...]
    ex = jnp.exp(lbp - jnp.max(lbp, axis=0, keepdims=True))
    sm = ex / jnp.sum(ex, axis=0, keepdims=True)
    lb = jnp.sum(sm[0:layer + 1, :], axis=0, keepdims=True) - sm[0:1, :]

    tri_f, tri_b = trif_ref[...], trib_ref[...]
    streams = []
    for hh in range(heads):
        ks = slice(hh * B_DK, (hh + 1) * B_DK)
        vs = slice(hh * B_DV, (hh + 1) * B_DV)
        streams.append(_Stream(q=qf_ref[:, ks].astype(F32), v=vf_ref[:, vs].astype(F32),
                               f=ff_ref[:, ks].astype(F32), lb=lb[:, ks], tri3=tri_f,
                               masks_ref=maskf_ref, st_ref=stf.at[hh], reverse=False))
        streams.append(_Stream(q=qb_ref[:, ks].astype(F32), v=vb_ref[:, vs].astype(F32),
                               f=fb_ref[:, ks].astype(F32), lb=lb[:, ks], tri3=tri_b,
                               masks_ref=maskb_ref, st_ref=stb.at[hh], reverse=True))
    _hgrn_chunk(streams)
    for hh in range(heads):
        vs = slice(hh * B_DV, (hh + 1) * B_DV)
        yf_ref[:, vs] = streams[2 * hh].o
        yb_ref[:, vs] = streams[2 * hh + 1].o


def _hgrn_masks(L):
    t = np.arange(L)[:, None]
    s = np.arange(L)[None, :]
    out = []
    for j in range(B_LEVELS):
        half = 1 << j
        c = 2 * half
        out.append(((t // c == s // c) & (t % c >= half) & (s % c < half)).astype(np.float32))
    return np.stack(out)


def hgrn_scan(p, lb_param, layer, batch, seq, heads=B_HEADS_PER_STEP):
    L = B_CHUNK
    nc = seq // L
    t = batch * seq
    ng = B_HEADS // heads
    tri_f = jnp.asarray(np.tile(np.tril(np.ones((L, L), np.float32)), (1, 3)), dtype=BF16)
    tri_b = jnp.asarray(np.tile(np.triu(np.ones((L, L), np.float32)), (1, 3)), dtype=BF16)
    masks_f = _hgrn_masks(L)
    masks_b = np.ascontiguousarray(np.transpose(masks_f, (0, 2, 1)))
    depth = lb_param.shape[0]

    fwd = lambda b, h, c: b * nc + c
    bwd = lambda b, h, c: b * nc + (nc - 1 - c)

    def dir_specs(row, fcol):
        return [
            pl.BlockSpec((L, heads * B_DK), lambda b, h, c: (row(b, h, c), h)),
            pl.BlockSpec((L, heads * B_DV), lambda b, h, c: (row(b, h, c), ng + h)),
            pl.BlockSpec((L, heads * B_DK), lambda b, h, c: (row(b, h, c), fcol * ng + h)),
        ]

    const2 = pl.BlockSpec((L, 3 * L), lambda b, h, c: (0, 0))
    const3 = pl.BlockSpec((B_LEVELS, L, L), lambda b, h, c: (0, 0, 0))
    in_specs = ([pl.BlockSpec((depth, heads * B_DK), lambda b, h, c: (0, h))]
                + dir_specs(fwd, 3) + dir_specs(bwd, 4) + [const2, const2, const3, const3])
    out_specs = [
        pl.BlockSpec((L, heads * B_DV), lambda b, h, c: (fwd(b, h, c), h)),
        pl.BlockSpec((L, heads * B_DV), lambda b, h, c: (bwd(b, h, c), h)),
    ]
    return pl.pallas_call(
        functools.partial(_hgrn_kernel, layer=layer, heads=heads),
        grid=(batch, ng, nc),
        in_specs=in_specs,
        out_specs=out_specs,
        out_shape=[jax.ShapeDtypeStruct((t, B_HEADS * B_DV), F32)] * 2,
        scratch_shapes=[pltpu.VMEM((heads, B_DV, B_DK), F32)] * 2,
        compiler_params=_params(("parallel", "parallel", "arbitrary")),
        name="hgrn_scan",
    )(lb_param, p, p, p, p, p, p, tri_f, tri_b, jnp.asarray(masks_f), jnp.asarray(masks_b))


def _mix_out_kernel(yf_ref, yb_ref, gate_ref, hg_ref, res_ref, w_ref, ng_ref, out_ref, hn_ref, w16, *, d_head, silu):
    d = res_ref.shape[1]

    @pl.when(pl.program_id(0) == 0)
    def _():
        for r0 in range(0, d, MXU_DIM):
            w16[r0:r0 + MXU_DIM, :] = w_ref[0, r0:r0 + MXU_DIM, :].astype(BF16)

    acc = res_ref[...]
    for c0 in range(0, d, MXU_DIM):
        parts = []
        for h0 in range(c0, c0 + MXU_DIM, d_head):
            sl = slice(h0, h0 + d_head)
            y = yf_ref[:, sl] + yb_ref[:, sl]
            ms = jnp.mean(y * y, axis=-1, keepdims=True)
            gate = gate_ref[:, sl].astype(F32)
            act = _sigmoid(gate)
            if silu:
                act = gate * act
            parts.append((y * lax.rsqrt(ms + EPS) * hg_ref[:, sl] * act).astype(BF16))
        z = parts[0] if len(parts) == 1 else jnp.concatenate(parts, axis=1)
        acc = acc + jnp.dot(z, w16[c0:c0 + MXU_DIM, :], preferred_element_type=F32)
    out_ref[...] = acc
    hn_ref[...] = _rms_rows(acc, ng_ref[...]).astype(hn_ref.dtype)


def mix_out(yf, yb, p, gate_col_block, head_g, res, w_out, layer, norm_g, d_head, silu, tm=256):
    t, d = yf.shape
    row = lambda i: (i, 0)
    const = lambda i: (0, 0)
    return pl.pallas_call(
        functools.partial(_mix_out_kernel, d_head=d_head, silu=silu),
        grid=(t // tm,),
        in_specs=[
            pl.BlockSpec((tm, d), row),
            pl.BlockSpec((tm, d), row),
            pl.BlockSpec((tm, d), lambda i: (i, gate_col_block)),
            pl.BlockSpec((1, d), const),
            pl.BlockSpec((tm, d), row),
            pl.BlockSpec((1, d, d), lambda i: (layer, 0, 0), pipeline_mode=pl.Buffered(1)),
            pl.BlockSpec((1, d), const),
        ],
        out_specs=[pl.BlockSpec((tm, d), row), pl.BlockSpec((tm, d), row)],
        out_shape=[jax.ShapeDtypeStruct((t, d), F32), jax.ShapeDtypeStruct((t, d), BF16)],
        scratch_shapes=[pltpu.VMEM((d, d), BF16)],
        compiler_params=_params(("arbitrary",)),
        name="mix_out",
    )(yf, yb, p, head_g.reshape(1, d), res, w_out, norm_g.reshape(1, d))


def _glu_up_kernel(hn_ref, wa0_ref, wa1_ref, wv0_ref, wv1_ref, cw_ref, cb_ref, g_ref, a_scr, v_scr, w16, *,
                   row_chunk):
    s = hn_ref.shape[0]
    tf = a_scr.shape[1]
    pad = SUBLANES
    zeros = jnp.zeros((pad, tf), F32)
    a_scr[0:pad, :] = zeros
    a_scr[pad + s:2 * pad + s, :] = zeros
    for i, w_ref in enumerate((wa0_ref, wa1_ref, wv0_ref, wv1_ref)):
        w16[:, i * LANES:(i + 1) * LANES] = w_ref[0].astype(BF16)

    def epilogue(first, last):
        n = last - first
        ext = a_scr[first:first + n + 2 * pad, :]
        prev = pltpu.roll(ext, 1, 0)[pad:pad + n, :]
        nxt = pltpu.roll(ext, n + 2 * pad - 1, 0)[pad:pad + n, :]
        c = (cb_ref[...] + prev * cw_ref[0:1, :] + ext[pad:pad + n, :] * cw_ref[1:2, :]
             + nxt * cw_ref[2:3, :])
        gelu = 0.5 * c * (1.0 + lax.erf(c * INV_SQRT2))
        g_ref[first:last, :] = (gelu * v_scr[first:last, :]).astype(BF16)

    trail = 2 * SUBLANES
    bounds = [0]
    while s - bounds[-1] > row_chunk:
        bounds.append(bounds[-1] + (s - bounds[-1]) // 2)
    bounds.append(s)
    done = 0
    for lo, hi in zip(bounds[:-1], bounds[1:]):
        if lo > 0:
            epilogue(done, lo - trail)
            done = lo - trail
        u = jnp.dot(hn_ref[lo:hi, :], w16[...], preferred_element_type=F32)
        a_scr[pad + lo:pad + hi, :] = u[:, :tf]
        v_scr[lo:hi, :] = u[:, tf:]
    epilogue(done, s)


def glu_up(hn, w_up, layer, conv_w, conv_b, batch, seq):
    t, d = hn.shape
    nf = conv_w.shape[1]
    tf = FF_TILE
    nb = D_FF // LANES
    last = 2 * nb - 1
    wspec = lambda off: pl.BlockSpec((1, d, LANES), lambda b, j: (layer, 0, jnp.minimum(off + 2 * j, last)))
    return pl.pallas_call(
        functools.partial(_glu_up_kernel, row_chunk=256),
        grid=(batch, nf // tf),
        in_specs=[
            pl.BlockSpec((seq, d), lambda b, j: (b, 0)),
            wspec(0), wspec(1), wspec(nb), wspec(nb + 1),
            pl.BlockSpec((3, tf), lambda b, j: (0, j)),
            pl.BlockSpec((1, tf), lambda b, j: (0, j)),
        ],
        out_specs=pl.BlockSpec((seq, tf), lambda b, j: (b, j)),
        out_shape=jax.ShapeDtypeStruct((t, D_FF), BF16),
        scratch_shapes=[pltpu.VMEM((seq + 2 * SUBLANES, tf), F32), pltpu.VMEM((seq, tf), F32),
                        pltpu.VMEM((d, 2 * tf), BF16)],
        compiler_params=_params(("parallel", "arbitrary")),
        name="glu_up",
    )(hn, w_up, w_up, w_up, w_up, conv_w, conv_b)


def _proj_res_kernel(x_ref, w_ref, res_ref, out_ref):
    out_ref[...] = res_ref[...] + jnp.dot(x_ref[...], w_ref[0], preferred_element_type=F32)


def proj_res(x, w, layer, res, tm=1024, tn=512):
    t, k = x.shape
    n = w.shape[2]
    return pl.pallas_call(
        _proj_res_kernel,
        grid=(t // tm, n // tn),
        in_specs=[
            pl.BlockSpec((tm, k), lambda i, j: (i, 0)),
            pl.BlockSpec((1, k, tn), lambda i, j: (layer, 0, j)),
            pl.BlockSpec((tm, tn), lambda i, j: (i, j)),
        ],
        out_specs=pl.BlockSpec((tm, tn), lambda i, j: (i, j)),
        out_shape=jax.ShapeDtypeStruct((t, n), F32),
        compiler_params=_params(("parallel", "arbitrary")),
        name="proj_res",
    )(x, w, res)


def conv_glu(h, hn, w_up, conv_w, conv_b, w_down16, layer, batch, seq):
    fpad = D_FF_PAD - D_FF
    cw = jnp.pad(conv_w[layer], ((0, 0), (0, fpad)))
    cb = jnp.pad(conv_b[layer], (0, fpad)).reshape(1, D_FF_PAD)
    g = glu_up(hn, w_up, layer, cw, cb, batch, seq)
    return proj_res(g, w_down16, layer, h)


def kernel(x, norm_mix_g, norm_ffn_g, mlstm_w_in, mlstm_b_gate, mlstm_head_g, mlstm_w_out,
           hgrn_w_in, hgrn_lb, hgrn_head_g, hgrn_w_out, ffn_w_up, ffn_conv_w, ffn_conv_b,
           ffn_w_down, final_g):
    batch, seq, d = x.shape
    h = x.reshape(batch * seq, d)
    depth = norm_mix_g.shape[0]
    w_down16 = ffn_w_down.astype(BF16)
    for layer in range(depth):
        j = layer // 2
        if layer % 2 == 0:
            n_main = 2 * A_HEADS * A_DK + 2 * A_HEADS * A_DV
            n_gate = mlstm_w_in.shape[2] - n_main
            w_gate = jnp.pad(mlstm_w_in[j, :, n_main:], ((0, 0), (0, LANES - n_gate))).astype(BF16)
            p, gates = norm_proj(h, norm_mix_g[layer], mlstm_w_in, j, n_main, F32, w_gate=w_gate)
            bias = jnp.pad(mlstm_b_gate[j], (0, LANES - n_gate))
            yf, yb = mlstm_scan(p, gates, bias, batch, seq)
            h, hn = mix_out(yf, yb, p, 2, mlstm_head_g[j], h, mlstm_w_out, j, norm_ffn_g[layer], A_DV,
                            silu=False)
        else:
            p = norm_proj(h, norm_mix_g[layer], hgrn_w_in, j, hgrn_w_in.shape[2], F32)
            yf, yb = hgrn_scan(p, hgrn_lb, layer, batch, seq)
            h, hn = mix_out(yf, yb, p, 2, hgrn_head_g[j], h, hgrn_w_out, j, norm_ffn_g[layer], B_DV,
                            silu=True)
        h = conv_glu(h, hn, ffn_w_up, ffn_conv_w, ffn_conv_b, w_down16, layer, batch, seq)
    return rmsnorm(h, final_g, x.dtype).reshape(batch, seq, d)
```

```python
import functools
import math

import numpy as np
import jax
import jax.numpy as jnp
from jax import lax
from jax.experimental import pallas as pl
from jax.experimental.pallas import tpu as pltpu

F32 = jnp.float32
BF16 = jnp.bfloat16

EPS = 1e-6
D_MODEL = 2048
LANES = 128
SUBLANES = 8
MXU_DIM = 256
VMEM_LIMIT_BYTES = 56 * 1024 * 1024

A_HEADS, A_DK, A_DV = 8, 128, 256
A_CHUNK = 256
A_HEADS_PER_STEP = 4
B_HEADS, B_DK, B_DV = 16, 128, 128
B_CHUNK = 128
B_LEVELS = 7
B_HEADS_PER_STEP = 8
D_FF = 5504
FF_TILE = 256
D_FF_PAD = 5632
INV_SQRT2 = 0.7071067811865476
LOG2E = 1.4426950408889634
MASKED = -1e30

_NT = (((1,), (1,)), ((), ()))


def _params(sem):
    return pltpu.CompilerParams(dimension_semantics=sem, vmem_limit_bytes=VMEM_LIMIT_BYTES)


def _rms_rows(x, g):
    ms = jnp.mean(x * x, axis=-1, keepdims=True)
    return x * lax.rsqrt(ms + EPS) * g


def _sigmoid(x):
    return 1.0 / (1.0 + jnp.exp(-x))


def _log_sigmoid(x):
    return jnp.minimum(x, 0.0) - jnp.log1p(jnp.exp(-jnp.abs(x)))


def _norm_kernel(x_ref, g_ref, o_ref):
    o_ref[...] = _rms_rows(x_ref[...], g_ref[...]).astype(o_ref.dtype)


def rmsnorm(x, g, out_dtype, tm=256):
    t, d = x.shape
    return pl.pallas_call(
        _norm_kernel,
        grid=(t // tm,),
        in_specs=[pl.BlockSpec((tm, d), lambda i: (i, 0)), pl.BlockSpec((1, d), lambda i: (0, 0))],
        out_specs=pl.BlockSpec((tm, d), lambda i: (i, 0)),
        out_shape=jax.ShapeDtypeStruct((t, d), out_dtype),
        compiler_params=_params(("parallel",)),
        name="rmsnorm",
    )(x, g.reshape(1, d))


def _norm_proj_kernel(x_ref, g_ref, w_ref, *rest, row_chunk, has_gate):
    if has_gate:
        wg_ref, o_ref, og_ref, hn_ref = rest
    else:
        o_ref, hn_ref = rest

    @pl.when(pl.program_id(1) == 0)
    def _():
        def body(r, carry):
            rows = pl.ds(pl.multiple_of(r * row_chunk, row_chunk), row_chunk)
            hn_ref[rows, :] = _rms_rows(x_ref[rows, :], g_ref[...]).astype(BF16)
            return carry

        lax.fori_loop(0, x_ref.shape[0] // row_chunk, body, 0)
        if has_gate:
            og_ref[...] = jnp.dot(hn_ref[...], wg_ref[...], preferred_element_type=F32)

    o_ref[...] = jnp.dot(hn_ref[...], w_ref[0].astype(BF16), preferred_element_type=F32).astype(o_ref.dtype)


def norm_proj(x, g, w, layer, n, out_dtype, w_gate=None, tm=1024, tn=512):
    t, d = x.shape
    has_gate = w_gate is not None
    in_specs = [
        pl.BlockSpec((tm, d), lambda i, j: (i, 0)),
        pl.BlockSpec((1, d), lambda i, j: (0, 0)),
        pl.BlockSpec((1, d, tn), lambda i, j: (layer, 0, j)),
    ]
    out_specs = [pl.BlockSpec((tm, tn), lambda i, j: (i, j))]
    out_shape = [jax.ShapeDtypeStruct((t, n), out_dtype)]
    args = [x, g.reshape(1, d), w]
    if has_gate:
        ng = w_gate.shape[1]
        in_specs.append(pl.BlockSpec((d, ng), lambda i, j: (0, 0)))
        out_specs.append(pl.BlockSpec((tm, ng), lambda i, j: (i, 0)))
        out_shape.append(jax.ShapeDtypeStruct((t, ng), F32))
        args.append(w_gate)
    outs = pl.pallas_call(
        functools.partial(_norm_proj_kernel, row_chunk=64, has_gate=has_gate),
        grid=(t // tm, n // tn),
        in_specs=in_specs,
        out_specs=out_specs,
        out_shape=out_shape,
        scratch_shapes=[pltpu.VMEM((tm, d), BF16)],
        compiler_params=_params(("parallel", "arbitrary")),
        name="norm_proj",
    )(*args)
    return outs if has_gate else outs[0]


class _Stream:
    def __init__(self, **kw):
        self.__dict__.update(kw)


def _sublane_scan(x, op, identity, reverse):
    L = x.shape[0]
    row = lax.broadcasted_iota(jnp.int32, x.shape, 0)
    k = 1
    while k < L:
        if reverse:
            shifted, valid = pltpu.roll(x, L - k, 0), row < L - k
        else:
            shifted, valid = pltpu.roll(x, k, 0), row >= k
        x = op(x, jnp.where(valid, shifted, identity))
        k *= 2
    return x


def _mlstm_gates(ig, f_pre, m_prev, reverse):
    L = ig.shape[0]
    ln_scale = -0.5 * math.log(A_DK)
    last = slice(0, 1) if reverse else slice(L - 1, L)
    b = _sublane_scan(_log_sigmoid(f_pre), jnp.add, 0.0, reverse)
    u = ig - b
    cm = _sublane_scan(u, jnp.maximum, -jnp.inf, reverse)
    mx = jnp.maximum(cm, m_prev)
    mn = jnp.maximum(cm[last, :], m_prev)
    u2 = u * LOG2E
    g = _Stream(
        cv2=(ln_scale - mx) * LOG2E,
        rv2t=jnp.concatenate([u2[i * LANES:(i + 1) * LANES, :].T for i in range(L // LANES)], axis=1),
        w_inter=jnp.exp(m_prev - mx + ln_scale),
        floor=jnp.exp(-mx - b),
        w_s=jnp.exp(u - mn),
        decay=jnp.exp(m_prev - mn),
        m_new=b[last, :] + mn)
    return g


def _mlstm_chunk(streams):
    for s in streams:
        qk = lax.dot_general(s.q, s.k, _NT, preferred_element_type=F32)
        s.p = qk * jnp.exp2(s.cv2 + s.rv2 + s.negmask)
    for s in streams:
        num = (jnp.dot(s.p.astype(BF16), s.v, preferred_element_type=F32)
               + s.w_inter * jnp.dot(s.q, s.ct_ref[...].astype(BF16), preferred_element_type=F32))
        den = (jnp.sum(s.p, axis=-1, keepdims=True)
               + s.w_inter * jnp.sum(s.q.astype(F32) * s.n_ref[...], axis=-1, keepdims=True))
        s.h = num * (1.0 / jnp.maximum(jnp.abs(den), s.floor))
    for s in streams:
        kw = s.k.astype(F32) * s.w_s
        s.ct_ref[...] = s.decay * s.ct_ref[...] + jnp.dot(kw.T.astype(BF16), s.v, preferred_element_type=F32)
        s.n_ref[...] = s.decay * s.n_ref[...] + jnp.sum(kw, axis=0, keepdims=True)


def _mlstm_kernel(bi_ref, bf_ref, qf_ref, kf_ref, vf_ref, gif_ref, gff_ref, qb_ref, kb_ref, vb_ref, gib_ref,
                  gfb_ref, negf_ref, negb_ref, yf_ref, yb_ref, ctf, nf, mf, ctb, nb, mb, *, heads):
    @pl.when(pl.program_id(2) == 0)
    def _():
        for ref in (ctf, nf, mf, ctb, nb, mb):
            ref[...] = jnp.zeros_like(ref)

    gates_f = _mlstm_gates(gif_ref[0, 0] + bi_ref[0], gff_ref[0, 0] + bf_ref[0], mf[...], False)
    gates_b = _mlstm_gates(gib_ref[0, 0] + bi_ref[0], gfb_ref[0, 0] + bf_ref[0], mb[...], True)
    mf[...] = gates_f.m_new
    mb[...] = gates_b.m_new

    streams = []
    for hh in range(heads):
        ks = slice(hh * A_DK, (hh + 1) * A_DK)
        vs = slice(hh * A_DV, (hh + 1) * A_DV)
        for g, lane, q_ref, k_ref, v_ref, neg_ref, ct, n in (
                (gates_f, hh, qf_ref, kf_ref, vf_ref, negf_ref, ctf, nf),
                (gates_b, heads + hh, qb_ref, kb_ref, vb_ref, negb_ref, ctb, nb)):
            col = slice(lane, lane + 1)
            streams.append(_Stream(
                q=q_ref[:, ks].astype(BF16), k=k_ref[:, ks].astype(BF16), v=v_ref[:, vs].astype(BF16),
                cv2=g.cv2[:, col], rv2=g.rv2t[col, :], w_inter=g.w_inter[:, col], floor=g.floor[:, col],
                w_s=g.w_s[:, col], decay=g.decay[:, col], negmask=neg_ref[...],
                ct_ref=ct.at[hh], n_ref=n.at[hh]))
    _mlstm_chunk(streams)
    for hh in range(heads):
        vs = slice(hh * A_DV, (hh + 1) * A_DV)
        yf_ref[:, vs] = streams[2 * hh].h.astype(yf_ref.dtype)
        yb_ref[:, vs] = streams[2 * hh + 1].h.astype(yb_ref.dtype)


def _group_gate_lanes(x, batch, seq, heads):
    ng = A_HEADS // heads
    x = x[:, :4 * A_HEADS].reshape(batch, seq, 2, 2, ng, heads)
    x = jnp.transpose(x, (3, 0, 4, 1, 2, 5)).reshape(2, batch, ng, seq, 2 * heads)
    x = jnp.pad(x, ((0, 0),) * 4 + ((0, LANES - 2 * heads),))
    return x[0], x[1]


def mlstm_scan(p, gates, bias, batch, seq, heads=A_HEADS_PER_STEP):
    L = A_CHUNK
    nc = seq // L
    t = batch * seq
    ng = A_HEADS // heads
    g_in, g_forget = _group_gate_lanes(gates, batch, seq, heads)
    b_in, b_forget = _group_gate_lanes(bias.reshape(1, -1), 1, 1, heads)
    keep = np.tril(np.ones((L, L), np.float32))
    neg_f = jnp.asarray((1.0 - keep) * MASKED)
    neg_b = jnp.asarray((1.0 - keep.T) * MASKED)

    fwd = lambda b, h, c: b * nc + c
    bwd = lambda b, h, c: b * nc + (nc - 1 - c)

    def dir_specs(row):
        gate = pl.BlockSpec((1, 1, L, LANES), lambda b, h, c: (b, h, row(b, h, c) - b * nc, 0))
        return [
            pl.BlockSpec((L, heads * A_DK), lambda b, h, c: (row(b, h, c), h)),
            pl.BlockSpec((L, heads * A_DK), lambda b, h, c: (row(b, h, c), ng + h)),
            pl.BlockSpec((L, heads * A_DV), lambda b, h, c: (row(b, h, c), ng + h)),
            gate, gate,
        ]

    const = pl.BlockSpec((L, L), lambda b, h, c: (0, 0))
    bias_spec = pl.BlockSpec((1, 1, LANES), lambda b, h, c: (h, 0, 0))
    in_specs = [bias_spec, bias_spec] + dir_specs(fwd) + dir_specs(bwd) + [const, const]
    out_specs = [
        pl.BlockSpec((L, heads * A_DV), lambda b, h, c: (fwd(b, h, c), h)),
        pl.BlockSpec((L, heads * A_DV), lambda b, h, c: (bwd(b, h, c), h)),
    ]
    state = [pltpu.VMEM((heads, A_DK, A_DV), F32), pltpu.VMEM((heads, 1, A_DK), F32),
             pltpu.VMEM((1, LANES), F32)]
    return pl.pallas_call(
        functools.partial(_mlstm_kernel, heads=heads),
        grid=(batch, ng, nc),
        in_specs=in_specs,
        out_specs=out_specs,
        out_shape=[jax.ShapeDtypeStruct((t, A_HEADS * A_DV), BF16)] * 2,
        scratch_shapes=state + state,
        compiler_params=_params(("parallel", "parallel", "arbitrary")),
        name="mlstm_scan",
    )(b_in[0], b_forget[0], p, p, p, g_in, g_forget, p, p, p, g_in, g_forget, neg_f, neg_b)


def _mid_rows(b, c, r_off):
    L = b.shape[0]
    if c >= SUBLANES:
        b3 = b.reshape(L // c, c, LANES)
        return jnp.broadcast_to(b3[:, r_off:r_off + 1, :], b3.shape).reshape(L, LANES)
    b3 = b.reshape(L // SUBLANES, SUBLANES, LANES)
    lo = jnp.broadcast_to(b3[:, r_off:r_off + 1, :], b3.shape)
    hi = jnp.broadcast_to(b3[:, c + r_off:c + r_off + 1, :], b3.shape)
    sub = lax.broadcasted_iota(jnp.int32, b3.shape, 1)
    return jnp.where(sub < c, lo, hi).reshape(L, LANES)


def _midpoint_exponent(b, half, reverse):
    L = b.shape[0]
    pieces = []
    for lo in range(0, L, 2 * half):
        first, second = b[lo:lo + half, :], b[lo + half:lo + 2 * half, :]
        if reverse:
            mid = b[lo + half:lo + half + 1, :]
            pieces += [first - mid, mid - second]
        else:
            mid = b[lo + half - 1:lo + half, :]
            pieces += [mid - first, second - mid]
    return jnp.concatenate(pieces, axis=0)


def _split3(x):
    hi = x.astype(BF16)
    r1 = x - hi.astype(F32)
    mid = r1.astype(BF16)
    lo = (r1 - mid.astype(F32)).astype(BF16)
    return hi, mid, lo


def _hgrn_chunk(streams):
    L = streams[0].q.shape[0]
    row = lax.broadcasted_iota(jnp.int32, (L, LANES), 0)
    odd = (row & 1) == 1

    for s in streams:
        e = jnp.exp(-jnp.abs(s.f))
        r = 1.0 / (1.0 + e)
        er = e * r
        pos = s.f >= 0.0
        s.forget = s.lb + (1.0 - s.lb) * jnp.where(pos, r, er)
        s.kk = (1.0 - s.lb) * jnp.where(pos, er, r)
    for s in streams:
        s.b = jnp.dot(s.tri3, jnp.concatenate(_split3(jnp.log2(s.forget)), axis=0),
                      preferred_element_type=F32)
        s.b_tot = s.b[0:1, :] if s.reverse else s.b[L - 1:L, :]
    for s in streams:
        s.st = s.st_ref[...]
        s.k16 = s.kk.astype(BF16)
        s.o = lax.dot_general(s.q * jnp.exp2(s.b).astype(BF16), s.st.astype(BF16), _NT,
                              preferred_element_type=F32)
        s.attn = None

    for j in range(-1, B_LEVELS):
        half = 1 << max(j, 0)
        for s in streams:
            if j < 0:
                lhs, rhs = s.q, s.k16
            elif j == 0:
                w16 = jnp.where(odd != s.reverse, s.forget, 1.0).astype(BF16)
                lhs, rhs = s.q * w16, s.k16 * w16
            else:
                if half < SUBLANES:
                    b_mid = _mid_rows(s.b, 2 * half, half if s.reverse else half - 1)
                    w16 = jnp.exp2(-jnp.abs(s.b - b_mid)).astype(BF16)
                else:
                    w16 = jnp.exp2(_midpoint_exponent(s.b, half, s.reverse)).astype(BF16)
                lhs, rhs = s.q * w16, s.k16 * w16
            pj = (lax.dot_general(lhs, rhs, _NT, preferred_element_type=F32).astype(BF16)
                  * s.masks_ref[j % (B_LEVELS + 1)])
            s.attn = pj if s.attn is None else s.attn + pj
    for s in streams:
        s.o = s.o + jnp.dot(s.attn, s.v, preferred_element_type=F32)
    for s in streams:
        kt = s.k16 * jnp.exp2(s.b_tot - s.b).astype(BF16)
        s.st_ref[...] = jnp.exp2(s.b_tot) * s.st + jnp.dot(s.v.astype(F32).T.astype(BF16), kt,
                                                           preferred_element_type=F32)


def _hgrn_kernel(lbp_ref, qf_ref, vf_ref, ff_ref, qb_ref, vb_ref, fb_ref,
                 trif_ref, trib_ref, maskf_ref, maskb_ref, yf_ref, yb_ref, stf, stb, *, layer, heads):
    @pl.when(pl.program_id(2) == 0)
    def _():
        stf[...] = jnp.zeros_like(stf)
        stb[...] = jnp.zeros_like(stb)

    lbp = lbp_ref[...]
    ex = jnp.exp(lbp - jnp.max(lbp, axis=0, keepdims=True))
    sm = ex / jnp.sum(ex, axis=0, keepdims=True)
    lb = jnp.sum(sm[0:layer + 1, :], axis=0, keepdims=True) - sm[0:1, :]

    tri_f, tri_b = trif_ref[...], trib_ref[...]
    streams = []
    for hh in range(heads):
        ks = slice(hh * B_DK, (hh + 1) * B_DK)
        vs = slice(hh * B_DV, (hh + 1) * B_DV)
        streams.append(_Stream(q=qf_ref[:, ks], v=vf_ref[:, vs], f=ff_ref[:, ks].astype(F32), lb=lb[:, ks],
                               tri3=tri_f, masks_ref=maskf_ref, st_ref=stf.at[hh], reverse=False))
        streams.append(_Stream(q=qb_ref[:, ks], v=vb_ref[:, vs], f=fb_ref[:, ks].astype(F32), lb=lb[:, ks],
                               tri3=tri_b, masks_ref=maskb_ref, st_ref=stb.at[hh], reverse=True))
    _hgrn_chunk(streams)
    for hh in range(heads):
        vs = slice(hh * B_DV, (hh + 1) * B_DV)
        yf_ref[:, vs] = streams[2 * hh].o.astype(yf_ref.dtype)
        yb_ref[:, vs] = streams[2 * hh + 1].o.astype(yb_ref.dtype)


def _hgrn_masks(L):
    t = np.arange(L)[:, None]
    s = np.arange(L)[None, :]
    out = []
    for j in range(B_LEVELS):
        half = 1 << j
        c = 2 * half
        out.append(((t // c == s // c) & (t % c >= half) & (s % c < half)).astype(np.float32))
    out.append(np.eye(L, dtype=np.float32))
    return np.stack(out)


def hgrn_scan(p, lb_param, layer, batch, seq, heads=B_HEADS_PER_STEP):
    L = B_CHUNK
    nc = seq // L
    t = batch * seq
    ng = B_HEADS // heads
    tri_f = jnp.asarray(np.tile(np.tril(np.ones((L, L), np.float32)), (1, 3)), dtype=BF16)
    tri_b = jnp.asarray(np.tile(np.triu(np.ones((L, L), np.float32)), (1, 3)), dtype=BF16)
    masks_f = _hgrn_masks(L)
    masks_b = np.ascontiguousarray(np.transpose(masks_f, (0, 2, 1)))
    depth = lb_param.shape[0]

    fwd = lambda b, h, c: b * nc + c
    bwd = lambda b, h, c: b * nc + (nc - 1 - c)

    def dir_specs(row, fcol):
        return [
            pl.BlockSpec((L, heads * B_DK), lambda b, h, c: (row(b, h, c), h)),
            pl.BlockSpec((L, heads * B_DV), lambda b, h, c: (row(b, h, c), ng + h)),
            pl.BlockSpec((L, heads * B_DK), lambda b, h, c: (row(b, h, c), fcol * ng + h)),
        ]

    const2 = pl.BlockSpec((L, 3 * L), lambda b, h, c: (0, 0))
    const3 = pl.BlockSpec((B_LEVELS + 1, L, L), lambda b, h, c: (0, 0, 0))
    in_specs = ([pl.BlockSpec((depth, heads * B_DK), lambda b, h, c: (0, h))]
                + dir_specs(fwd, 3) + dir_specs(bwd, 4) + [const2, const2, const3, const3])
    out_specs = [
        pl.BlockSpec((L, heads * B_DV), lambda b, h, c: (fwd(b, h, c), h)),
        pl.BlockSpec((L, heads * B_DV), lambda b, h, c: (bwd(b, h, c), h)),
    ]
    return pl.pallas_call(
        functools.partial(_hgrn_kernel, layer=layer, heads=heads),
        grid=(batch, ng, nc),
        in_specs=in_specs,
        out_specs=out_specs,
        out_shape=[jax.ShapeDtypeStruct((t, B_HEADS * B_DV), BF16)] * 2,
        scratch_shapes=[pltpu.VMEM((heads, B_DV, B_DK), F32)] * 2,
        compiler_params=_params(("parallel", "parallel", "arbitrary")),
        name="hgrn_scan",
    )(lb_param, p, p, p, p, p, p, tri_f, tri_b, jnp.asarray(masks_f, dtype=BF16), jnp.asarray(masks_b, dtype=BF16))


def _mix_out_kernel(yf_ref, yb_ref, gate_ref, hg_ref, res_ref, w_ref, ng_ref, out_ref, hn_ref, w16, *, d_head, silu):
    d = res_ref.shape[1]

    @pl.when(pl.program_id(0) == 0)
    def _():
        for r0 in range(0, d, MXU_DIM):
            w16[r0:r0 + MXU_DIM, :] = w_ref[0, r0:r0 + MXU_DIM, :].astype(BF16)

    acc = res_ref[...]
    for c0 in range(0, d, MXU_DIM):
        parts = []
        for h0 in range(c0, c0 + MXU_DIM, d_head):
            sl = slice(h0, h0 + d_head)
            y = yf_ref[:, sl].astype(F32) + yb_ref[:, sl].astype(F32)
            ms = jnp.mean(y * y, axis=-1, keepdims=True)
            gate = gate_ref[:, sl].astype(F32)
            act = _sigmoid(gate)
            if silu:
                act = gate * act
            parts.append((y * lax.rsqrt(ms + EPS) * hg_ref[:, sl] * act).astype(BF16))
        z = parts[0] if len(parts) == 1 else jnp.concatenate(parts, axis=1)
        acc = acc + jnp.dot(z, w16[c0:c0 + MXU_DIM, :], preferred_element_type=F32)
    out_ref[...] = acc
    hn_ref[...] = _rms_rows(acc, ng_ref[...]).astype(hn_ref.dtype)


def mix_out(yf, yb, p, gate_col_block, head_g, res, w_out, layer, norm_g, d_head, silu, tm=256):
    t, d = yf.shape
    row = lambda i: (i, 0)
    const = lambda i: (0, 0)
    return pl.pallas_call(
        functools.partial(_mix_out_kernel, d_head=d_head, silu=silu),
        grid=(t // tm,),
        in_specs=[
            pl.BlockSpec((tm, d), row),
            pl.BlockSpec((tm, d), row),
            pl.BlockSpec((tm, d), lambda i: (i, gate_col_block)),
            pl.BlockSpec((1, d), const),
            pl.BlockSpec((tm, d), row),
            pl.BlockSpec((1, d, d), lambda i: (layer, 0, 0), pipeline_mode=pl.Buffered(1)),
            pl.BlockSpec((1, d), const),
        ],
        out_specs=[pl.BlockSpec((tm, d), row), pl.BlockSpec((tm, d), row)],
        out_shape=[jax.ShapeDtypeStruct((t, d), F32), jax.ShapeDtypeStruct((t, d), BF16)],
        scratch_shapes=[pltpu.VMEM((d, d), BF16)],
        compiler_params=_params(("arbitrary",)),
        name="mix_out",
    )(yf, yb, p, head_g.reshape(1, d), res, w_out, norm_g.reshape(1, d))


def _glu_up_kernel(hn_ref, wa0_ref, wa1_ref, wv0_ref, wv1_ref, cw_ref, cb_ref, g_ref, a_scr, v_scr, w16, *,
                   row_chunk):
    s = hn_ref.shape[0]
    tf = a_scr.shape[1]
    pad = SUBLANES
    zeros = jnp.zeros((pad, tf), F32)
    a_scr[0:pad, :] = zeros
    a_scr[pad + s:2 * pad + s, :] = zeros
    for i, w_ref in enumerate((wa0_ref, wa1_ref, wv0_ref, wv1_ref)):
        w16[:, i * LANES:(i + 1) * LANES] = w_ref[0].astype(BF16)

    def epilogue(first, last):
        n = last - first
        ext = a_scr[first:first + n + 2 * pad, :]
        prev = pltpu.roll(ext, 1, 0)[pad:pad + n, :]
        nxt = pltpu.roll(ext, n + 2 * pad - 1, 0)[pad:pad + n, :]
        c = (cb_ref[...] + prev * cw_ref[0:1, :] + ext[pad:pad + n, :] * cw_ref[1:2, :]
             + nxt * cw_ref[2:3, :])
        gelu = 0.5 * c * (1.0 + lax.erf(c * INV_SQRT2))
        g_ref[first:last, :] = (gelu * v_scr[first:last, :]).astype(BF16)

    trail = 2 * SUBLANES
    bounds = [0]
    while s - bounds[-1] > row_chunk:
        bounds.append(bounds[-1] + (s - bounds[-1]) // 2)
    bounds.append(s)
    done = 0
    for lo, hi in zip(bounds[:-1], bounds[1:]):
        if lo > 0:
            epilogue(done, lo - trail)
            done = lo - trail
        u = jnp.dot(hn_ref[lo:hi, :], w16[...], preferred_element_type=F32)
        a_scr[pad + lo:pad + hi, :] = u[:, :tf]
        v_scr[lo:hi, :] = u[:, tf:]
    epilogue(done, s)


def glu_up(hn, w_up, layer, conv_w, conv_b, batch, seq):
    t, d = hn.shape
    nf = conv_w.shape[1]
    tf = FF_TILE
    nb = D_FF // LANES
    last = 2 * nb - 1
    wspec = lambda off: pl.BlockSpec((1, d, LANES), lambda b, j: (layer, 0, jnp.minimum(off + 2 * j, last)))
    return pl.pallas_call(
        functools.partial(_glu_up_kernel, row_chunk=256),
        grid=(batch, nf // tf),
        in_specs=[
            pl.BlockSpec((seq, d), lambda b, j: (b, 0)),
            wspec(0), wspec(1), wspec(nb), wspec(nb + 1),
            pl.BlockSpec((3, tf), lambda b, j: (0, j)),
            pl.BlockSpec((1, tf), lambda b, j: (0, j)),
        ],
        out_specs=pl.BlockSpec((seq, tf), lambda b, j: (b, j)),
        out_shape=jax.ShapeDtypeStruct((t, D_FF), BF16),
        scratch_shapes=[pltpu.VMEM((seq + 2 * SUBLANES, tf), F32), pltpu.VMEM((seq, tf), F32),
                        pltpu.VMEM((d, 2 * tf), BF16)],
        compiler_params=_params(("parallel", "arbitrary")),
        name="glu_up",
    )(hn, w_up, w_up, w_up, w_up, conv_w, conv_b)


def _proj_res_kernel(x_ref, w_ref, res_ref, out_ref):
    out_ref[...] = res_ref[...] + jnp.dot(x_ref[...], w_ref[0], preferred_element_type=F32)


def proj_res(x, w, layer, res, tm=1024, tn=512):
    t, k = x.shape
    n = w.shape[2]
    return pl.pallas_call(
        _proj_res_kernel,
        grid=(t // tm, n // tn),
        in_specs=[
            pl.BlockSpec((tm, k), lambda i, j: (i, 0)),
            pl.BlockSpec((1, k, tn), lambda i, j: (layer, 0, j)),
            pl.BlockSpec((tm, tn), lambda i, j: (i, j)),
        ],
        out_specs=pl.BlockSpec((tm, tn), lambda i, j: (i, j)),
        out_shape=jax.ShapeDtypeStruct((t, n), F32),
        compiler_params=_params(("parallel", "arbitrary")),
        name="proj_res",
    )(x, w, res)


def conv_glu(h, hn, w_up, conv_w, conv_b, w_down16, layer, batch, seq):
    fpad = D_FF_PAD - D_FF
    cw = jnp.pad(conv_w[layer], ((0, 0), (0, fpad)))
    cb = jnp.pad(conv_b[layer], (0, fpad)).reshape(1, D_FF_PAD)
    g = glu_up(hn, w_up, layer, cw, cb, batch, seq)
    return proj_res(g, w_down16, layer, h)


def kernel(x, norm_mix_g, norm_ffn_g, mlstm_w_in, mlstm_b_gate, mlstm_head_g, mlstm_w_out,
           hgrn_w_in, hgrn_lb, hgrn_head_g, hgrn_w_out, ffn_w_up, ffn_conv_w, ffn_conv_b,
           ffn_w_down, final_g):
    batch, seq, d = x.shape
    h = x.reshape(batch * seq, d)
    depth = norm_mix_g.shape[0]
    w_down16 = ffn_w_down.astype(BF16)
    for layer in range(depth):
        j = layer // 2
        if layer % 2 == 0:
            n_main = 2 * A_HEADS * A_DK + 2 * A_HEADS * A_DV
            n_gate = mlstm_w_in.shape[2] - n_main
            w_gate = jnp.pad(mlstm_w_in[j, :, n_main:], ((0, 0), (0, LANES - n_gate))).astype(BF16)
            p, gates = norm_proj(h, norm_mix_g[layer], mlstm_w_in, j, n_main, BF16, w_gate=w_gate)
            bias = jnp.pad(mlstm_b_gate[j], (0, LANES - n_gate))
            yf, yb = mlstm_scan(p, gates, bias, batch, seq)
            h, hn = mix_out(yf, yb, p, 2, mlstm_head_g[j], h, mlstm_w_out, j, norm_ffn_g[layer], A_DV,
                            silu=False)
        else:
            p = norm_proj(h, norm_mix_g[layer], hgrn_w_in, j, hgrn_w_in.shape[2], BF16)
            yf, yb = hgrn_scan(p, hgrn_lb, layer, batch, seq)
            h, hn = mix_out(yf, yb, p, 2, hgrn_head_g[j], h, hgrn_w_out, j, norm_ffn_g[layer], B_DV,
                            silu=True)
        h = conv_glu(h, hn, ffn_w_up, ffn_conv_w, ffn_conv_b, w_down16, layer, batch, seq)
    return rmsnorm(h, final_g, x.dtype).reshape(batch, seq, d)
```

```python
import functools
import math

import numpy as np
import jax
import jax.numpy as jnp
from jax import lax
from jax.experimental import pallas as pl
from jax.experimental.pallas import tpu as pltpu

F32 = jnp.float32
BF16 = jnp.bfloat16

EPS = 1e-6
D_MODEL = 2048
LANES = 128
SUBLANES = 8
MXU_DIM = 256
VMEM_LIMIT_BYTES = 56 * 1024 * 1024

A_HEADS, A_DK, A_DV = 8, 128, 256
A_CHUNK = 256
A_HEADS_PER_STEP = 4
B_HEADS, B_DK, B_DV = 16, 128, 128
B_CHUNK = 128
B_LEVELS = 7
B_HEADS_PER_STEP = 4
D_FF = 5504
FF_TILE = 256
D_FF_PAD = 5632
HN_PAD = LANES
INV_SQRT2 = 0.7071067811865476
LOG2E = 1.4426950408889634
MASKED = -1e30

_NT = (((1,), (1,)), ((), ()))


def _params(sem):
    return pltpu.CompilerParams(dimension_semantics=sem, vmem_limit_bytes=VMEM_LIMIT_BYTES)


def _rms_rows(x, g):
    ms = jnp.mean(x * x, axis=-1, keepdims=True)
    return x * lax.rsqrt(ms + EPS) * g


def _sigmoid(x):
    return 1.0 / (1.0 + jnp.exp(-x))


def _log_sigmoid(x):
    return jnp.minimum(x, 0.0) - jnp.log1p(jnp.exp(-jnp.abs(x)))


def _norm_kernel(x_ref, g_ref, o_ref):
    o_ref[...] = _rms_rows(x_ref[...], g_ref[...]).astype(o_ref.dtype)


def rmsnorm(x, g, out_dtype, tm=256):
    t, d = x.shape
    return pl.pallas_call(
        _norm_kernel,
        grid=(t // tm,),
        in_specs=[pl.BlockSpec((tm, d), lambda i: (i, 0)), pl.BlockSpec((1, d), lambda i: (0, 0))],
        out_specs=pl.BlockSpec((tm, d), lambda i: (i, 0)),
        out_shape=jax.ShapeDtypeStruct((t, d), out_dtype),
        compiler_params=_params(("parallel",)),
        name="rmsnorm",
    )(x, g.reshape(1, d))


def _norm_proj_kernel(x_ref, g_ref, w_ref, *rest, row_chunk, has_gate):
    if has_gate:
        wg_ref, o_ref, og_ref, hn_ref = rest
    else:
        o_ref, hn_ref = rest

    @pl.when(pl.program_id(1) == 0)
    def _():
        def body(r, carry):
            rows = pl.ds(pl.multiple_of(r * row_chunk, row_chunk), row_chunk)
            hn_ref[rows, :] = _rms_rows(x_ref[rows, :], g_ref[...]).astype(BF16)
            return carry

        lax.fori_loop(0, x_ref.shape[0] // row_chunk, body, 0)
        if has_gate:
            og_ref[...] = jnp.dot(hn_ref[...], wg_ref[...], preferred_element_type=F32)

    o_ref[...] = jnp.dot(hn_ref[...], w_ref[0].astype(BF16), preferred_element_type=F32).astype(o_ref.dtype)


def norm_proj(x, g, w, layer, n, out_dtype, w_gate=None, tm=2048, tn=512):
    t, d = x.shape
    has_gate = w_gate is not None
    in_specs = [
        pl.BlockSpec((tm, d), lambda i, j: (i, 0), pipeline_mode=pl.Buffered(1)),
        pl.BlockSpec((1, d), lambda i, j: (0, 0)),
        pl.BlockSpec((1, d, tn), lambda i, j: (layer, 0, j)),
    ]
    out_specs = [pl.BlockSpec((tm, tn), lambda i, j: (i, j))]
    out_shape = [jax.ShapeDtypeStruct((t, n), out_dtype)]
    args = [x, g.reshape(1, d), w]
    if has_gate:
        ng = w_gate.shape[1]
        in_specs.append(pl.BlockSpec((d, ng), lambda i, j: (0, 0)))
        out_specs.append(pl.BlockSpec((tm, ng), lambda i, j: (i, 0)))
        out_shape.append(jax.ShapeDtypeStruct((t, ng), F32))
        args.append(w_gate)
    outs = pl.pallas_call(
        functools.partial(_norm_proj_kernel, row_chunk=64, has_gate=has_gate),
        grid=(t // tm, n // tn),
        in_specs=in_specs,
        out_specs=out_specs,
        out_shape=out_shape,
        scratch_shapes=[pltpu.VMEM((tm, d), BF16)],
        compiler_params=_params(("parallel", "arbitrary")),
        name="norm_proj",
    )(*args)
    return outs if has_gate else outs[0]


class _Stream:
    def __init__(self, **kw):
        self.__dict__.update(kw)


def _sublane_scan(x, op, identity, reverse):
    L = x.shape[0]
    row = lax.broadcasted_iota(jnp.int32, x.shape, 0)
    k = 1
    while k < L:
        if reverse:
            shifted, valid = pltpu.roll(x, L - k, 0), row < L - k
        else:
            shifted, valid = pltpu.roll(x, k, 0), row >= k
        x = op(x, jnp.where(valid, shifted, identity))
        k *= 2
    return x


def _mlstm_gates(ig, f_pre, m_prev, reverse):
    L = ig.shape[0]
    ln_scale = -0.5 * math.log(A_DK)
    last = slice(0, 1) if reverse else slice(L - 1, L)
    b = _sublane_scan(_log_sigmoid(f_pre), jnp.add, 0.0, reverse)
    u = ig - b
    cm = _sublane_scan(u, jnp.maximum, -jnp.inf, reverse)
    mx = jnp.maximum(cm, m_prev)
    mn = jnp.maximum(cm[last, :], m_prev)
    u2 = u * LOG2E
    g = _Stream(
        cv2=(ln_scale - mx) * LOG2E,
        rv2t=jnp.concatenate([u2[i * LANES:(i + 1) * LANES, :].T for i in range(L // LANES)], axis=1),
        w_inter=jnp.exp(m_prev - mx + ln_scale),
        floor=jnp.exp(-mx - b),
        w_s=jnp.exp(u - mn),
        decay=jnp.exp(m_prev - mn),
        m_new=b[last, :] + mn)
    return g


def _mlstm_chunk(streams):
    for s in streams:
        qk = lax.dot_general(s.q, s.k, _NT, preferred_element_type=F32)
        s.p = qk * jnp.exp2(s.cv2 + s.rv2 + s.negmask)
    for s in streams:
        num = (jnp.dot(s.p.astype(BF16), s.v, preferred_element_type=F32)
               + s.w_inter * jnp.dot(s.q, s.ct_ref[...].astype(BF16), preferred_element_type=F32))
        den = (jnp.sum(s.p, axis=-1, keepdims=True)
               + s.w_inter * jnp.sum(s.q.astype(F32) * s.n_ref[...], axis=-1, keepdims=True))
        s.h = num * (1.0 / jnp.maximum(jnp.abs(den), s.floor))
    for s in streams:
        kw = s.k.astype(F32) * s.w_s
        s.ct_ref[...] = s.decay * s.ct_ref[...] + jnp.dot(kw.T.astype(BF16), s.v, preferred_element_type=F32)
        s.n_ref[...] = s.decay * s.n_ref[...] + jnp.sum(kw, axis=0, keepdims=True)


def _mlstm_kernel(bi_ref, bf_ref, qf_ref, kf_ref, vf_ref, gif_ref, gff_ref, qb_ref, kb_ref, vb_ref, gib_ref,
                  gfb_ref, negf_ref, negb_ref, yf_ref, yb_ref, ctf, nf, mf, ctb, nb, mb, *, heads):
    @pl.when(pl.program_id(2) == 0)
    def _():
        for ref in (ctf, nf, mf, ctb, nb, mb):
            ref[...] = jnp.zeros_like(ref)

    gates_f = _mlstm_gates(gif_ref[0, 0] + bi_ref[0], gff_ref[0, 0] + bf_ref[0], mf[...], False)
    gates_b = _mlstm_gates(gib_ref[0, 0] + bi_ref[0], gfb_ref[0, 0] + bf_ref[0], mb[...], True)
    mf[...] = gates_f.m_new
    mb[...] = gates_b.m_new

    streams = []
    for hh in range(heads):
        ks = slice(hh * A_DK, (hh + 1) * A_DK)
        vs = slice(hh * A_DV, (hh + 1) * A_DV)
        for g, lane, q_ref, k_ref, v_ref, neg_ref, ct, n in (
                (gates_f, hh, qf_ref, kf_ref, vf_ref, negf_ref, ctf, nf),
                (gates_b, heads + hh, qb_ref, kb_ref, vb_ref, negb_ref, ctb, nb)):
            col = slice(lane, lane + 1)
            streams.append(_Stream(
                q=q_ref[:, ks].astype(BF16), k=k_ref[:, ks].astype(BF16), v=v_ref[:, vs].astype(BF16),
                cv2=g.cv2[:, col], rv2=g.rv2t[col, :], w_inter=g.w_inter[:, col], floor=g.floor[:, col],
                w_s=g.w_s[:, col], decay=g.decay[:, col], negmask=neg_ref[...],
                ct_ref=ct.at[hh], n_ref=n.at[hh]))
    _mlstm_chunk(streams)
    for hh in range(heads):
        vs = slice(hh * A_DV, (hh + 1) * A_DV)
        yf_ref[:, vs] = streams[2 * hh].h.astype(yf_ref.dtype)
        yb_ref[:, vs] = streams[2 * hh + 1].h.astype(yb_ref.dtype)


def _group_gate_lanes(x, batch, seq, heads):
    ng = A_HEADS // heads
    x = x[:, :4 * A_HEADS].reshape(batch, seq, 2, 2, ng, heads)
    x = jnp.transpose(x, (3, 0, 4, 1, 2, 5)).reshape(2, batch, ng, seq, 2 * heads)
    x = jnp.pad(x, ((0, 0),) * 4 + ((0, LANES - 2 * heads),))
    return x[0], x[1]


def mlstm_scan(p, gates, bias, batch, seq, heads=A_HEADS_PER_STEP):
    L = A_CHUNK
    nc = seq // L
    t = batch * seq
    ng = A_HEADS // heads
    g_in, g_forget = _group_gate_lanes(gates, batch, seq, heads)
    b_in, b_forget = _group_gate_lanes(bias.reshape(1, -1), 1, 1, heads)
    keep = np.tril(np.ones((L, L), np.float32))
    neg_f = jnp.asarray((1.0 - keep) * MASKED)
    neg_b = jnp.asarray((1.0 - keep.T) * MASKED)

    fwd = lambda b, h, c: b * nc + c
    bwd = lambda b, h, c: b * nc + (nc - 1 - c)

    def dir_specs(row):
        gate = pl.BlockSpec((1, 1, L, LANES), lambda b, h, c: (b, h, row(b, h, c) - b * nc, 0))
        return [
            pl.BlockSpec((L, heads * A_DK), lambda b, h, c: (row(b, h, c), h)),
            pl.BlockSpec((L, heads * A_DK), lambda b, h, c: (row(b, h, c), ng + h)),
            pl.BlockSpec((L, heads * A_DV), lambda b, h, c: (row(b, h, c), ng + h)),
            gate, gate,
        ]

    const = pl.BlockSpec((L, L), lambda b, h, c: (0, 0))
    bias_spec = pl.BlockSpec((1, 1, LANES), lambda b, h, c: (h, 0, 0))
    in_specs = [bias_spec, bias_spec] + dir_specs(fwd) + dir_specs(bwd) + [const, const]
    out_specs = [
        pl.BlockSpec((L, heads * A_DV), lambda b, h, c: (fwd(b, h, c), h)),
        pl.BlockSpec((L, heads * A_DV), lambda b, h, c: (bwd(b, h, c), h)),
    ]
    state = [pltpu.VMEM((heads, A_DK, A_DV), F32), pltpu.VMEM((heads, 1, A_DK), F32),
             pltpu.VMEM((1, LANES), F32)]
    return pl.pallas_call(
        functools.partial(_mlstm_kernel, heads=heads),
        grid=(batch, ng, nc),
        in_specs=in_specs,
        out_specs=out_specs,
        out_shape=[jax.ShapeDtypeStruct((t, A_HEADS * A_DV), BF16)] * 2,
        scratch_shapes=state + state,
        compiler_params=_params(("parallel", "parallel", "arbitrary")),
        name="mlstm_scan",
    )(b_in[0], b_forget[0], p, p, p, g_in, g_forget, p, p, p, g_in, g_forget, neg_f, neg_b)


def _mid_rows(b, c, r_off):
    L = b.shape[0]
    if c >= SUBLANES:
        b3 = b.reshape(L // c, c, LANES)
        return jnp.broadcast_to(b3[:, r_off:r_off + 1, :], b3.shape).reshape(L, LANES)
    b3 = b.reshape(L // SUBLANES, SUBLANES, LANES)
    lo = jnp.broadcast_to(b3[:, r_off:r_off + 1, :], b3.shape)
    hi = jnp.broadcast_to(b3[:, c + r_off:c + r_off + 1, :], b3.shape)
    sub = lax.broadcasted_iota(jnp.int32, b3.shape, 1)
    return jnp.where(sub < c, lo, hi).reshape(L, LANES)


def _midpoint_exponent(b, half, reverse):
    L = b.shape[0]
    pieces = []
    for lo in range(0, L, 2 * half):
        first, second = b[lo:lo + half, :], b[lo + half:lo + 2 * half, :]
        if reverse:
            mid = b[lo + half:lo + half + 1, :]
            pieces += [first - mid, mid - second]
        else:
            mid = b[lo + half - 1:lo + half, :]
            pieces += [mid - first, second - mid]
    return jnp.concatenate(pieces, axis=0)


def _split3(x):
    hi = x.astype(BF16)
    r1 = x - hi.astype(F32)
    mid = r1.astype(BF16)
    lo = (r1 - mid.astype(F32)).astype(BF16)
    return hi, mid, lo


def _hgrn_chunk(streams):
    L = streams[0].q.shape[0]
    row = lax.broadcasted_iota(jnp.int32, (L, LANES), 0)
    odd = (row & 1) == 1

    for s in streams:
        e = jnp.exp(-jnp.abs(s.f))
        r = 1.0 / (1.0 + e)
        er = e * r
        pos = s.f >= 0.0
        s.forget = s.lb + (1.0 - s.lb) * jnp.where(pos, r, er)
        s.kk = (1.0 - s.lb) * jnp.where(pos, er, r)
    for s in streams:
        s.b = jnp.dot(s.tri3, jnp.concatenate(_split3(jnp.log2(s.forget)), axis=0),
                      preferred_element_type=F32)
        s.b_tot = s.b[0:1, :] if s.reverse else s.b[L - 1:L, :]
    for s in streams:
        s.st = s.st_ref[...]
        s.k16 = s.kk.astype(BF16)
        s.o = lax.dot_general(s.q * jnp.exp2(s.b).astype(BF16), s.st.astype(BF16), _NT,
                              preferred_element_type=F32)
        s.attn = None

    for j in range(-1, B_LEVELS):
        half = 1 << max(j, 0)
        for s in streams:
            if j < 0:
                lhs, rhs = s.q, s.k16
            elif j == 0:
                w16 = jnp.where(odd != s.reverse, s.forget, 1.0).astype(BF16)
                lhs, rhs = s.q * w16, s.k16 * w16
            else:
                if half < SUBLANES:
                    b_mid = _mid_rows(s.b, 2 * half, half if s.reverse else half - 1)
                    w16 = jnp.exp2(-jnp.abs(s.b - b_mid)).astype(BF16)
                else:
                    w16 = jnp.exp2(_midpoint_exponent(s.b, half, s.reverse)).astype(BF16)
                lhs, rhs = s.q * w16, s.k16 * w16
            pj = (lax.dot_general(lhs, rhs, _NT, preferred_element_type=F32).astype(BF16)
                  * s.masks_ref[j % (B_LEVELS + 1)])
            s.attn = pj if s.attn is None else s.attn + pj
    for s in streams:
        s.o = s.o + jnp.dot(s.attn, s.v, preferred_element_type=F32)
    for s in streams:
        kt = s.k16 * jnp.exp2(s.b_tot - s.b).astype(BF16)
        s.st_ref[...] = jnp.exp2(s.b_tot) * s.st + jnp.dot(s.v.astype(F32).T.astype(BF16), kt,
                                                           preferred_element_type=F32)


def _hgrn_kernel(lbp_ref, qf_ref, vf_ref, ff_ref, qb_ref, vb_ref, fb_ref,
                 trif_ref, trib_ref, maskf_ref, maskb_ref, yf_ref, yb_ref, stf, stb, *, layer, heads):
    @pl.when(pl.program_id(2) == 0)
    def _():
        stf[...] = jnp.zeros_like(stf)
        stb[...] = jnp.zeros_like(stb)

    lbp = lbp_ref[...]
    ex = jnp.exp(lbp - jnp.max(lbp, axis=0, keepdims=True))
    sm = ex / jnp.sum(ex, axis=0, keepdims=True)
    lb = jnp.sum(sm[0:layer + 1, :], axis=0, keepdims=True) - sm[0:1, :]

    tri_f, tri_b = trif_ref[...], trib_ref[...]
    streams = []
    for hh in range(heads):
        ks = slice(hh * B_DK, (hh + 1) * B_DK)
        vs = slice(hh * B_DV, (hh + 1) * B_DV)
        streams.append(_Stream(q=qf_ref[:, ks], v=vf_ref[:, vs], f=ff_ref[:, ks].astype(F32), lb=lb[:, ks],
                               tri3=tri_f, masks_ref=maskf_ref, st_ref=stf.at[hh], reverse=False))
        streams.append(_Stream(q=qb_ref[:, ks], v=vb_ref[:, vs], f=fb_ref[:, ks].astype(F32), lb=lb[:, ks],
                               tri3=tri_b, masks_ref=maskb_ref, st_ref=stb.at[hh], reverse=True))
    _hgrn_chunk(streams)
    for hh in range(heads):
        vs = slice(hh * B_DV, (hh + 1) * B_DV)
        yf_ref[:, vs] = streams[2 * hh].o.astype(yf_ref.dtype)
        yb_ref[:, vs] = streams[2 * hh + 1].o.astype(yb_ref.dtype)


def _hgrn_masks(L):
    t = np.arange(L)[:, None]
    s = np.arange(L)[None, :]
    out = []
    for j in range(B_LEVELS):
        half = 1 << j
        c = 2 * half
        out.append(((t // c == s // c) & (t % c >= half) & (s % c < half)).astype(np.float32))
    out.append(np.eye(L, dtype=np.float32))
    return np.stack(out)


def hgrn_scan(p, lb_param, layer, batch, seq, heads=B_HEADS_PER_STEP):
    L = B_CHUNK
    nc = seq // L
    t = batch * seq
    ng = B_HEADS // heads
    tri_f = jnp.asarray(np.tile(np.tril(np.ones((L, L), np.float32)), (1, 3)), dtype=BF16)
    tri_b = jnp.asarray(np.tile(np.triu(np.ones((L, L), np.float32)), (1, 3)), dtype=BF16)
    masks_f = _hgrn_masks(L)
    masks_b = np.ascontiguousarray(np.transpose(masks_f, (0, 2, 1)))
    depth = lb_param.shape[0]

    fwd = lambda b, h, c: b * nc + c
    bwd = lambda b, h, c: b * nc + (nc - 1 - c)

    def dir_specs(row, fcol):
        return [
            pl.BlockSpec((L, heads * B_DK), lambda b, h, c: (row(b, h, c), h)),
            pl.BlockSpec((L, heads * B_DV), lambda b, h, c: (row(b, h, c), ng + h)),
            pl.BlockSpec((L, heads * B_DK), lambda b, h, c: (row(b, h, c), fcol * ng + h)),
        ]

    const2 = pl.BlockSpec((L, 3 * L), lambda b, h, c: (0, 0))
    const3 = pl.BlockSpec((B_LEVELS + 1, L, L), lambda b, h, c: (0, 0, 0))
    in_specs = ([pl.BlockSpec((depth, heads * B_DK), lambda b, h, c: (0, h))]
                + dir_specs(fwd, 3) + dir_specs(bwd, 4) + [const2, const2, const3, const3])
    out_specs = [
        pl.BlockSpec((L, heads * B_DV), lambda b, h, c: (fwd(b, h, c), h)),
        pl.BlockSpec((L, heads * B_DV), lambda b, h, c: (bwd(b, h, c), h)),
    ]
    return pl.pallas_call(
        functools.partial(_hgrn_kernel, layer=layer, heads=heads),
        grid=(batch, ng, nc),
        in_specs=in_specs,
        out_specs=out_specs,
        out_shape=[jax.ShapeDtypeStruct((t, B_HEADS * B_DV), BF16)] * 2,
        scratch_shapes=[pltpu.VMEM((heads, B_DV, B_DK), F32)] * 2,
        compiler_params=_params(("parallel", "parallel", "arbitrary")),
        name="hgrn_scan",
    )(lb_param, p, p, p, p, p, p, tri_f, tri_b, jnp.asarray(masks_f, dtype=BF16), jnp.asarray(masks_b, dtype=BF16))


def _mix_out_kernel(yf_ref, yb_ref, gate_ref, hg_ref, res_ref, w_ref, ng_ref, out_ref, hn_ref, w16, *, d_head, silu):
    d = res_ref.shape[1]

    @pl.when(pl.program_id(0) == 0)
    def _():
        for r0 in range(0, d, MXU_DIM):
            w16[r0:r0 + MXU_DIM, :] = w_ref[0, r0:r0 + MXU_DIM, :].astype(BF16)

    acc = res_ref[...]
    for c0 in range(0, d, MXU_DIM):
        parts = []
        for h0 in range(c0, c0 + MXU_DIM, d_head):
            sl = slice(h0, h0 + d_head)
            y = yf_ref[:, sl].astype(F32) + yb_ref[:, sl].astype(F32)
            ms = jnp.mean(y * y, axis=-1, keepdims=True)
            gate = gate_ref[:, sl].astype(F32)
            act = _sigmoid(gate)
            if silu:
                act = gate * act
            parts.append((y * lax.rsqrt(ms + EPS) * hg_ref[:, sl] * act).astype(BF16))
        z = parts[0] if len(parts) == 1 else jnp.concatenate(parts, axis=1)
        acc = acc + jnp.dot(z, w16[c0:c0 + MXU_DIM, :], preferred_element_type=F32)
    out_ref[...] = acc
    hn_ref[:, :d] = _rms_rows(acc, ng_ref[...]).astype(hn_ref.dtype)
    hn_ref[:, d:] = jnp.zeros((hn_ref.shape[0], hn_ref.shape[1] - d), hn_ref.dtype)


def mix_out(yf, yb, p, gate_col_block, head_g, res, w_out, layer, norm_g, d_head, silu, tm=256):
    t, d = yf.shape
    row = lambda i: (i, 0)
    const = lambda i: (0, 0)
    return pl.pallas_call(
        functools.partial(_mix_out_kernel, d_head=d_head, silu=silu),
        grid=(t // tm,),
        in_specs=[
            pl.BlockSpec((tm, d), row),
            pl.BlockSpec((tm, d), row),
            pl.BlockSpec((tm, d), lambda i: (i, gate_col_block)),
            pl.BlockSpec((1, d), const),
            pl.BlockSpec((tm, d), row),
            pl.BlockSpec((1, d, d), lambda i: (layer, 0, 0), pipeline_mode=pl.Buffered(1)),
            pl.BlockSpec((1, d), const),
        ],
        out_specs=[pl.BlockSpec((tm, d), row), pl.BlockSpec((tm, d + HN_PAD), row)],
        out_shape=[jax.ShapeDtypeStruct((t, d), F32), jax.ShapeDtypeStruct((t, d + HN_PAD), BF16)],
        scratch_shapes=[pltpu.VMEM((d, d), BF16)],
        compiler_params=_params(("arbitrary",)),
        name="mix_out",
    )(yf, yb, p, head_g.reshape(1, d), res, w_out, norm_g.reshape(1, d))


def _glu_up_kernel(hn_ref, wa0_ref, wa1_ref, wv0_ref, wv1_ref, cw_ref, cb_ref, g_ref, a_scr, v_scr, w16, *,
                   row_chunk):
    s = hn_ref.shape[0]
    d = w16.shape[0]
    tf = a_scr.shape[1]
    pad = SUBLANES
    zeros = jnp.zeros((pad, tf), F32)
    a_scr[0:pad, :] = zeros
    a_scr[pad + s:2 * pad + s, :] = zeros
    for i, w_ref in enumerate((wa0_ref, wa1_ref, wv0_ref, wv1_ref)):
        w16[:, i * LANES:(i + 1) * LANES] = w_ref[0].astype(BF16)

    def epilogue(first, last):
        n = last - first
        ext = a_scr[first:first + n + 2 * pad, :]
        prev = pltpu.roll(ext, 1, 0)[pad:pad + n, :]
        nxt = pltpu.roll(ext, n + 2 * pad - 1, 0)[pad:pad + n, :]
        c = (cb_ref[...] + prev * cw_ref[0:1, :] + ext[pad:pad + n, :] * cw_ref[1:2, :]
             + nxt * cw_ref[2:3, :])
        gelu = 0.5 * c * (1.0 + lax.erf(c * INV_SQRT2))
        g_ref[first:last, :] = (gelu * v_scr[first:last, :]).astype(BF16)

    trail = 2 * SUBLANES
    bounds = [0]
    while s - bounds[-1] > row_chunk:
        bounds.append(bounds[-1] + (s - bounds[-1]) // 2)
    bounds.append(s)
    done = 0
    for lo, hi in zip(bounds[:-1], bounds[1:]):
        if lo > 0:
            epilogue(done, lo - trail)
            done = lo - trail
        u = jnp.dot(hn_ref[lo:hi, :d], w16[...], preferred_element_type=F32)
        a_scr[pad + lo:pad + hi, :] = u[:, :tf]
        v_scr[lo:hi, :] = u[:, tf:]
    epilogue(done, s)


def glu_up(hn, w_up, layer, conv_w, conv_b, batch, seq):
    t = hn.shape[0]
    d = w_up.shape[1]
    nf = conv_w.shape[1]
    tf = FF_TILE
    nb = D_FF // LANES
    last = 2 * nb - 1
    wspec = lambda off: pl.BlockSpec((1, d, LANES), lambda b, j: (layer, 0, jnp.minimum(off + 2 * j, last)))
    return pl.pallas_call(
        functools.partial(_glu_up_kernel, row_chunk=256),
        grid=(batch, nf // tf),
        in_specs=[
            pl.BlockSpec((seq, hn.shape[1]), lambda b, j: (b, 0)),
            wspec(0), wspec(1), wspec(nb), wspec(nb + 1),
            pl.BlockSpec((3, tf), lambda b, j: (0, j)),
            pl.BlockSpec((1, tf), lambda b, j: (0, j)),
        ],
        out_specs=pl.BlockSpec((seq, tf), lambda b, j: (b, j)),
        out_shape=jax.ShapeDtypeStruct((t, D_FF), BF16),
        scratch_shapes=[pltpu.VMEM((seq + 2 * SUBLANES, tf), F32), pltpu.VMEM((seq, tf), F32),
                        pltpu.VMEM((d, 2 * tf), BF16)],
        compiler_params=_params(("parallel", "arbitrary")),
        name="glu_up",
    )(hn, w_up, w_up, w_up, w_up, conv_w, conv_b)


def _proj_res_kernel(x_ref, w_ref, res_ref, out_ref):
    out_ref[...] = res_ref[...] + jnp.dot(x_ref[...], w_ref[0], preferred_element_type=F32)


def proj_res(x, w, layer, res, tm=1024, tn=512):
    t, k = x.shape
    n = w.shape[2]
    return pl.pallas_call(
        _proj_res_kernel,
        grid=(t // tm, n // tn),
        in_specs=[
            pl.BlockSpec((tm, k), lambda i, j: (i, 0)),
            pl.BlockSpec((1, k, tn), lambda i, j: (layer, 0, j)),
            pl.BlockSpec((tm, tn), lambda i, j: (i, j)),
        ],
        out_specs=pl.BlockSpec((tm, tn), lambda i, j: (i, j)),
        out_shape=jax.ShapeDtypeStruct((t, n), F32),
        compiler_params=_params(("parallel", "arbitrary")),
        name="proj_res",
    )(x, w, res)


def conv_glu(h, hn, w_up, conv_w, conv_b, w_down16, layer, batch, seq):
    fpad = D_FF_PAD - D_FF
    cw = jnp.pad(conv_w[layer], ((0, 0), (0, fpad)))
    cb = jnp.pad(conv_b[layer], (0, fpad)).reshape(1, D_FF_PAD)
    g = glu_up(hn, w_up, layer, cw, cb, batch, seq)
    return proj_res(g, w_down16, layer, h)


def kernel(x, norm_mix_g, norm_ffn_g, mlstm_w_in, mlstm_b_gate, mlstm_head_g, mlstm_w_out,
           hgrn_w_in, hgrn_lb, hgrn_head_g, hgrn_w_out, ffn_w_up, ffn_conv_w, ffn_conv_b,
           ffn_w_down, final_g):
    batch, seq, d = x.shape
    h = x.reshape(batch * seq, d)
    depth = norm_mix_g.shape[0]
    w_down16 = ffn_w_down.astype(BF16)
    for layer in range(depth):
        j = layer // 2
        if layer % 2 == 0:
            n_main = 2 * A_HEADS * A_DK + 2 * A_HEADS * A_DV
            n_gate = mlstm_w_in.shape[2] - n_main
            w_gate = jnp.pad(mlstm_w_in[j, :, n_main:], ((0, 0), (0, LANES - n_gate))).astype(BF16)
            p, gates = norm_proj(h, norm_mix_g[layer], mlstm_w_in, j, n_main, BF16, w_gate=w_gate)
            bias = jnp.pad(mlstm_b_gate[j], (0, LANES - n_gate))
            yf, yb = mlstm_scan(p, gates, bias, batch, seq)
            h, hn = mix_out(yf, yb, p, 2, mlstm_head_g[j], h, mlstm_w_out, j, norm_ffn_g[layer], A_DV,
                            silu=False)
        else:
            p = norm_proj(h, norm_mix_g[layer], hgrn_w_in, j, hgrn_w_in.shape[2], BF16)
            yf, yb = hgrn_scan(p, hgrn_lb, layer, batch, seq)
            h, hn = mix_out(yf, yb, p, 2, hgrn_head_g[j], h, hgrn_w_out, j, norm_ffn_g[layer], B_DV,
                            silu=True)
        h = conv_glu(h, hn, ffn_w_up, ffn_conv_w, ffn_conv_b, w_down16, layer, batch, seq)
    return rmsnorm(h, final_g, x.dtype).reshape(batch, seq, d)
```

```python
import functools
import math

import numpy as np
import jax
import jax.numpy as jnp
from jax import lax
from jax.experimental import pallas as pl
from jax.experimental.pallas import tpu as pltpu

F32 = jnp.float32
BF16 = jnp.bfloat16

EPS = 1e-6
D_MODEL = 2048
LANES = 128
SUBLANES = 8
MXU_DIM = 256
VMEM_LIMIT_BYTES = 56 * 1024 * 1024

A_HEADS, A_DK, A_DV = 8, 128, 256
A_CHUNK = 256
A_HEADS_PER_STEP = 4
B_HEADS, B_DK, B_DV = 16, 128, 128
B_CHUNK = 128
B_LEVELS = 7
B_HEADS_PER_STEP = 4
D_FF = 5504
FF_TILE = 256
D_FF_PAD = 5632
HN_PAD = LANES
INV_SQRT2 = 0.7071067811865476
LOG2E = 1.4426950408889634
MASKED = -1e30

_NT = (((1,), (1,)), ((), ()))


def _params(sem):
    return pltpu.CompilerParams(dimension_semantics=sem, vmem_limit_bytes=VMEM_LIMIT_BYTES)


def _rms_rows(x, g):
    ms = jnp.mean(x * x, axis=-1, keepdims=True)
    return x * lax.rsqrt(ms + EPS) * g


def _sigmoid(x):
    return 1.0 / (1.0 + jnp.exp(-x))


def _log_sigmoid(x):
    return jnp.minimum(x, 0.0) - jnp.log1p(jnp.exp(-jnp.abs(x)))


def _norm_kernel(x_ref, g_ref, o_ref):
    o_ref[...] = _rms_rows(x_ref[...], g_ref[...]).astype(o_ref.dtype)


def rmsnorm(x, g, out_dtype, tm=256):
    t, d = x.shape
    return pl.pallas_call(
        _norm_kernel,
        grid=(t // tm,),
        in_specs=[pl.BlockSpec((tm, d), lambda i: (i, 0)), pl.BlockSpec((1, d), lambda i: (0, 0))],
        out_specs=pl.BlockSpec((tm, d), lambda i: (i, 0)),
        out_shape=jax.ShapeDtypeStruct((t, d), out_dtype),
        compiler_params=_params(("parallel",)),
        name="rmsnorm",
    )(x, g.reshape(1, d))


def _norm_proj_kernel(x_ref, g_ref, w_ref, *rest, row_chunk, has_gate):
    if has_gate:
        wg_ref, o_ref, og_ref, hn_ref = rest
    else:
        o_ref, hn_ref = rest

    @pl.when(pl.program_id(1) == 0)
    def _():
        def body(r, carry):
            rows = pl.ds(pl.multiple_of(r * row_chunk, row_chunk), row_chunk)
            hn_ref[rows, :] = _rms_rows(x_ref[rows, :], g_ref[...]).astype(BF16)
            return carry

        lax.fori_loop(0, x_ref.shape[0] // row_chunk, body, 0)
        if has_gate:
            og_ref[...] = jnp.dot(hn_ref[...], wg_ref[...], preferred_element_type=F32)

    o_ref[...] = jnp.dot(hn_ref[...], w_ref[0].astype(BF16), preferred_element_type=F32).astype(o_ref.dtype)


def norm_proj(x, g, w, layer, n, out_dtype, w_gate=None, tm=2048, tn=512):
    t, d = x.shape
    has_gate = w_gate is not None
    in_specs = [
        pl.BlockSpec((tm, d), lambda i, j: (i, 0), pipeline_mode=pl.Buffered(1)),
        pl.BlockSpec((1, d), lambda i, j: (0, 0)),
        pl.BlockSpec((1, d, tn), lambda i, j: (layer, 0, j)),
    ]
    out_specs = [pl.BlockSpec((tm, tn), lambda i, j: (i, j))]
    out_shape = [jax.ShapeDtypeStruct((t, n), out_dtype)]
    args = [x, g.reshape(1, d), w]
    if has_gate:
        ng = w_gate.shape[1]
        in_specs.append(pl.BlockSpec((d, ng), lambda i, j: (0, 0)))
        out_specs.append(pl.BlockSpec((tm, ng), lambda i, j: (i, 0)))
        out_shape.append(jax.ShapeDtypeStruct((t, ng), F32))
        args.append(w_gate)
    outs = pl.pallas_call(
        functools.partial(_norm_proj_kernel, row_chunk=64, has_gate=has_gate),
        grid=(t // tm, n // tn),
        in_specs=in_specs,
        out_specs=out_specs,
        out_shape=out_shape,
        scratch_shapes=[pltpu.VMEM((tm, d), BF16)],
        compiler_params=_params(("parallel", "arbitrary")),
        name="norm_proj",
    )(*args)
    return outs if has_gate else outs[0]


class _Stream:
    def __init__(self, **kw):
        self.__dict__.update(kw)


def _sublane_scan(x, op, identity, reverse):
    L = x.shape[0]
    row = lax.broadcasted_iota(jnp.int32, x.shape, 0)
    k = 1
    while k < L:
        if reverse:
            shifted, valid = pltpu.roll(x, L - k, 0), row < L - k
        else:
            shifted, valid = pltpu.roll(x, k, 0), row >= k
        x = op(x, jnp.where(valid, shifted, identity))
        k *= 2
    return x


def _mlstm_gates(ig, f_pre, m_prev, reverse):
    L = ig.shape[0]
    ln_scale = -0.5 * math.log(A_DK)
    last = slice(0, 1) if reverse else slice(L - 1, L)
    b = _sublane_scan(_log_sigmoid(f_pre), jnp.add, 0.0, reverse)
    u = ig - b
    cm = _sublane_scan(u, jnp.maximum, -jnp.inf, reverse)
    mx = jnp.maximum(cm, m_prev)
    mn = jnp.maximum(cm[last, :], m_prev)
    u2 = u * LOG2E
    g = _Stream(
        cv2=(ln_scale - mx) * LOG2E,
        rv2t=jnp.concatenate([u2[i * LANES:(i + 1) * LANES, :].T for i in range(L // LANES)], axis=1),
        w_inter=jnp.exp(m_prev - mx + ln_scale),
        floor=jnp.exp(-mx - b),
        w_s=jnp.exp(u - mn),
        decay=jnp.exp(m_prev - mn),
        m_new=b[last, :] + mn)
    return g


def _mlstm_chunk(streams):
    for s in streams:
        qk = lax.dot_general(s.q, s.k, _NT, preferred_element_type=F32)
        s.p = qk * jnp.exp2(s.cv2 + s.rv2 + s.negmask)
    for s in streams:
        num = (jnp.dot(s.p.astype(BF16), s.v, preferred_element_type=F32)
               + s.w_inter * jnp.dot(s.q, s.ct_ref[...].astype(BF16), preferred_element_type=F32))
        den = (jnp.sum(s.p, axis=-1, keepdims=True)
               + s.w_inter * jnp.sum(s.q.astype(F32) * s.n_ref[...], axis=-1, keepdims=True))
        s.h = num * (1.0 / jnp.maximum(jnp.abs(den), s.floor))
    for s in streams:
        kw = s.k.astype(F32) * s.w_s
        s.ct_ref[...] = s.decay * s.ct_ref[...] + jnp.dot(kw.T.astype(BF16), s.v, preferred_element_type=F32)
        s.n_ref[...] = s.decay * s.n_ref[...] + jnp.sum(kw, axis=0, keepdims=True)


def _mlstm_kernel(bi_ref, bf_ref, qf_ref, kf_ref, vf_ref, gif_ref, gff_ref, qb_ref, kb_ref, vb_ref, gib_ref,
                  gfb_ref, negf_ref, negb_ref, yf_ref, yb_ref, ctf, nf, mf, ctb, nb, mb, *, heads):
    @pl.when(pl.program_id(2) == 0)
    def _():
        for ref in (ctf, nf, mf, ctb, nb, mb):
            ref[...] = jnp.zeros_like(ref)

    gates_f = _mlstm_gates(gif_ref[0, 0] + bi_ref[0], gff_ref[0, 0] + bf_ref[0], mf[...], False)
    gates_b = _mlstm_gates(gib_ref[0, 0] + bi_ref[0], gfb_ref[0, 0] + bf_ref[0], mb[...], True)
    mf[...] = gates_f.m_new
    mb[...] = gates_b.m_new

    streams = []
    for hh in range(heads):
        ks = slice(hh * A_DK, (hh + 1) * A_DK)
        vs = slice(hh * A_DV, (hh + 1) * A_DV)
        for g, lane, q_ref, k_ref, v_ref, neg_ref, ct, n in (
                (gates_f, hh, qf_ref, kf_ref, vf_ref, negf_ref, ctf, nf),
                (gates_b, heads + hh, qb_ref, kb_ref, vb_ref, negb_ref, ctb, nb)):
            col = slice(lane, lane + 1)
            streams.append(_Stream(
                q=q_ref[:, ks].astype(BF16), k=k_ref[:, ks].astype(BF16), v=v_ref[:, vs].astype(BF16),
                cv2=g.cv2[:, col], rv2=g.rv2t[col, :], w_inter=g.w_inter[:, col], floor=g.floor[:, col],
                w_s=g.w_s[:, col], decay=g.decay[:, col], negmask=neg_ref[...],
                ct_ref=ct.at[hh], n_ref=n.at[hh]))
    _mlstm_chunk(streams)
    for hh in range(heads):
        vs = slice(hh * A_DV, (hh + 1) * A_DV)
        yf_ref[:, vs] = streams[2 * hh].h.astype(yf_ref.dtype)
        yb_ref[:, vs] = streams[2 * hh + 1].h.astype(yb_ref.dtype)


def _group_gate_lanes(x, batch, seq, heads):
    ng = A_HEADS // heads
    x = x[:, :4 * A_HEADS].reshape(batch, seq, 2, 2, ng, heads)
    x = jnp.transpose(x, (3, 0, 4, 1, 2, 5)).reshape(2, batch, ng, seq, 2 * heads)
    x = jnp.pad(x, ((0, 0),) * 4 + ((0, LANES - 2 * heads),))
    return x[0], x[1]


def mlstm_scan(p, gates, bias, batch, seq, heads=A_HEADS_PER_STEP):
    L = A_CHUNK
    nc = seq // L
    t = batch * seq
    ng = A_HEADS // heads
    g_in, g_forget = _group_gate_lanes(gates, batch, seq, heads)
    b_in, b_forget = _group_gate_lanes(bias.reshape(1, -1), 1, 1, heads)
    keep = np.tril(np.ones((L, L), np.float32))
    neg_f = jnp.asarray((1.0 - keep) * MASKED)
    neg_b = jnp.asarray((1.0 - keep.T) * MASKED)

    fwd = lambda b, h, c: b * nc + c
    bwd = lambda b, h, c: b * nc + (nc - 1 - c)

    def dir_specs(row):
        gate = pl.BlockSpec((1, 1, L, LANES), lambda b, h, c: (b, h, row(b, h, c) - b * nc, 0))
        return [
            pl.BlockSpec((L, heads * A_DK), lambda b, h, c: (row(b, h, c), h)),
            pl.BlockSpec((L, heads * A_DK), lambda b, h, c: (row(b, h, c), ng + h)),
            pl.BlockSpec((L, heads * A_DV), lambda b, h, c: (row(b, h, c), ng + h)),
            gate, gate,
        ]

    const = pl.BlockSpec((L, L), lambda b, h, c: (0, 0))
    bias_spec = pl.BlockSpec((1, 1, LANES), lambda b, h, c: (h, 0, 0))
    in_specs = [bias_spec, bias_spec] + dir_specs(fwd) + dir_specs(bwd) + [const, const]
    out_specs = [
        pl.BlockSpec((L, heads * A_DV), lambda b, h, c: (fwd(b, h, c), h)),
        pl.BlockSpec((L, heads * A_DV), lambda b, h, c: (bwd(b, h, c), h)),
    ]
    state = [pltpu.VMEM((heads, A_DK, A_DV), F32), pltpu.VMEM((heads, 1, A_DK), F32),
             pltpu.VMEM((1, LANES), F32)]
    return pl.pallas_call(
        functools.partial(_mlstm_kernel, heads=heads),
        grid=(batch, ng, nc),
        in_specs=in_specs,
        out_specs=out_specs,
        out_shape=[jax.ShapeDtypeStruct((t, A_HEADS * A_DV), BF16)] * 2,
        scratch_shapes=state + state,
        compiler_params=_params(("parallel", "parallel", "arbitrary")),
        name="mlstm_scan",
    )(b_in[0], b_forget[0], p, p, p, g_in, g_forget, p, p, p, g_in, g_forget, neg_f, neg_b)


def _mid_rows(b, c, r_off):
    L = b.shape[0]
    if c >= SUBLANES:
        b3 = b.reshape(L // c, c, LANES)
        return jnp.broadcast_to(b3[:, r_off:r_off + 1, :], b3.shape).reshape(L, LANES)
    b3 = b.reshape(L // SUBLANES, SUBLANES, LANES)
    lo = jnp.broadcast_to(b3[:, r_off:r_off + 1, :], b3.shape)
    hi = jnp.broadcast_to(b3[:, c + r_off:c + r_off + 1, :], b3.shape)
    sub = lax.broadcasted_iota(jnp.int32, b3.shape, 1)
    return jnp.where(sub < c, lo, hi).reshape(L, LANES)


def _midpoint_exponent(b, half, reverse):
    L = b.shape[0]
    pieces = []
    for lo in range(0, L, 2 * half):
        first, second = b[lo:lo + half, :], b[lo + half:lo + 2 * half, :]
        if reverse:
            mid = b[lo + half:lo + half + 1, :]
            pieces += [first - mid, mid - second]
        else:
            mid = b[lo + half - 1:lo + half, :]
            pieces += [mid - first, second - mid]
    return jnp.concatenate(pieces, axis=0)


def _split3(x):
    hi = x.astype(BF16)
    r1 = x - hi.astype(F32)
    mid = r1.astype(BF16)
    lo = (r1 - mid.astype(F32)).astype(BF16)
    return hi, mid, lo


def _hgrn_chunk(streams):
    L = streams[0].q.shape[0]
    row = lax.broadcasted_iota(jnp.int32, (L, LANES), 0)
    odd = (row & 1) == 1

    for s in streams:
        e = jnp.exp(-jnp.abs(s.f))
        r = 1.0 / (1.0 + e)
        er = e * r
        pos = s.f >= 0.0
        s.forget = s.lb + (1.0 - s.lb) * jnp.where(pos, r, er)
        s.kk = (1.0 - s.lb) * jnp.where(pos, er, r)
    for s in streams:
        s.b = jnp.dot(s.tri3, jnp.concatenate(_split3(jnp.log2(s.forget)), axis=0),
                      preferred_element_type=F32)
        s.b_tot = s.b[0:1, :] if s.reverse else s.b[L - 1:L, :]
    for s in streams:
        s.st = s.st_ref[...]
        s.q32 = s.q.astype(F32)
        s.o = lax.dot_general((s.q32 * jnp.exp2(s.b)).astype(BF16), s.st.astype(BF16), _NT,
                              preferred_element_type=F32)
        s.attn = None

    for j in range(-1, B_LEVELS):
        half = 1 << max(j, 0)
        for s in streams:
            if j < 0:
                lhs, rhs = s.q, s.kk.astype(BF16)
            else:
                if j == 0:
                    w = jnp.where(odd != s.reverse, s.forget, 1.0)
                elif half < SUBLANES:
                    b_mid = _mid_rows(s.b, 2 * half, half if s.reverse else half - 1)
                    w = jnp.exp2(-jnp.abs(s.b - b_mid))
                else:
                    w = jnp.exp2(_midpoint_exponent(s.b, half, s.reverse))
                lhs, rhs = (s.q32 * w).astype(BF16), (s.kk * w).astype(BF16)
            pj = lax.dot_general(lhs, rhs, _NT, preferred_element_type=F32) * s.masks_ref[j % (B_LEVELS + 1)]
            s.attn = pj if s.attn is None else s.attn + pj
    for s in streams:
        s.o = s.o + jnp.dot(s.attn.astype(BF16), s.v, preferred_element_type=F32)
    for s in streams:
        kt = (s.kk * jnp.exp2(s.b_tot - s.b)).astype(BF16)
        s.st_ref[...] = jnp.exp2(s.b_tot) * s.st + jnp.dot(s.v.astype(F32).T.astype(BF16), kt,
                                                           preferred_element_type=F32)


def _hgrn_kernel(lbp_ref, qf_ref, vf_ref, ff_ref, qb_ref, vb_ref, fb_ref,
                 trif_ref, trib_ref, maskf_ref, maskb_ref, yf_ref, yb_ref, stf, stb, *, layer, heads):
    @pl.when(pl.program_id(2) == 0)
    def _():
        stf[...] = jnp.zeros_like(stf)
        stb[...] = jnp.zeros_like(stb)

    lbp = lbp_ref[...]
    ex = jnp.exp(lbp - jnp.max(lbp, axis=0, keepdims=True))
    sm = ex / jnp.sum(ex, axis=0, keepdims=True)
    lb = jnp.sum(sm[0:layer + 1, :], axis=0, keepdims=True) - sm[0:1, :]

    tri_f, tri_b = trif_ref[...], trib_ref[...]
    streams = []
    for hh in range(heads):
        ks = slice(hh * B_DK, (hh + 1) * B_DK)
        vs = slice(hh * B_DV, (hh + 1) * B_DV)
        streams.append(_Stream(q=qf_ref[:, ks], v=vf_ref[:, vs], f=ff_ref[:, ks].astype(F32), lb=lb[:, ks],
                               tri3=tri_f, masks_ref=maskf_ref, st_ref=stf.at[hh], reverse=False))
        streams.append(_Stream(q=qb_ref[:, ks], v=vb_ref[:, vs], f=fb_ref[:, ks].astype(F32), lb=lb[:, ks],
                               tri3=tri_b, masks_ref=maskb_ref, st_ref=stb.at[hh], reverse=True))
    _hgrn_chunk(streams)
    for hh in range(heads):
        vs = slice(hh * B_DV, (hh + 1) * B_DV)
        yf_ref[:, vs] = streams[2 * hh].o.astype(yf_ref.dtype)
        yb_ref[:, vs] = streams[2 * hh + 1].o.astype(yb_ref.dtype)


def _hgrn_masks(L):
    t = np.arange(L)[:, None]
    s = np.arange(L)[None, :]
    out = []
    for j in range(B_LEVELS):
        half = 1 << j
        c = 2 * half
        out.append(((t // c == s // c) & (t % c >= half) & (s % c < half)).astype(np.float32))
    out.append(np.eye(L, dtype=np.float32))
    return np.stack(out)


def hgrn_scan(p, lb_param, layer, batch, seq, heads=B_HEADS_PER_STEP):
    L = B_CHUNK
    nc = seq // L
    t = batch * seq
    ng = B_HEADS // heads
    tri_f = jnp.asarray(np.tile(np.tril(np.ones((L, L), np.float32)), (1, 3)), dtype=BF16)
    tri_b = jnp.asarray(np.tile(np.triu(np.ones((L, L), np.float32)), (1, 3)), dtype=BF16)
    masks_f = _hgrn_masks(L)
    masks_b = np.ascontiguousarray(np.transpose(masks_f, (0, 2, 1)))
    depth = lb_param.shape[0]

    fwd = lambda b, h, c: b * nc + c
    bwd = lambda b, h, c: b * nc + (nc - 1 - c)

    def dir_specs(row, fcol):
        return [
            pl.BlockSpec((L, heads * B_DK), lambda b, h, c: (row(b, h, c), h)),
            pl.BlockSpec((L, heads * B_DV), lambda b, h, c: (row(b, h, c), ng + h)),
            pl.BlockSpec((L, heads * B_DK), lambda b, h, c: (row(b, h, c), fcol * ng + h)),
        ]

    const2 = pl.BlockSpec((L, 3 * L), lambda b, h, c: (0, 0))
    const3 = pl.BlockSpec((B_LEVELS + 1, L, L), lambda b, h, c: (0, 0, 0))
    in_specs = ([pl.BlockSpec((depth, heads * B_DK), lambda b, h, c: (0, h))]
                + dir_specs(fwd, 3) + dir_specs(bwd, 4) + [const2, const2, const3, const3])
    out_specs = [
        pl.BlockSpec((L, heads * B_DV), lambda b, h, c: (fwd(b, h, c), h)),
        pl.BlockSpec((L, heads * B_DV), lambda b, h, c: (bwd(b, h, c), h)),
    ]
    return pl.pallas_call(
        functools.partial(_hgrn_kernel, layer=layer, heads=heads),
        grid=(batch, ng, nc),
        in_specs=in_specs,
        out_specs=out_specs,
        out_shape=[jax.ShapeDtypeStruct((t, B_HEADS * B_DV), BF16)] * 2,
        scratch_shapes=[pltpu.VMEM((heads, B_DV, B_DK), F32)] * 2,
        compiler_params=_params(("parallel", "parallel", "arbitrary")),
        name="hgrn_scan",
    )(lb_param, p, p, p, p, p, p, tri_f, tri_b, jnp.asarray(masks_f), jnp.asarray(masks_b))


def _mix_out_kernel(yf_ref, yb_ref, gate_ref, hg_ref, res_ref, w_ref, ng_ref, out_ref, hn_ref, w16, *, d_head, silu):
    d = res_ref.shape[1]

    @pl.when(pl.program_id(0) == 0)
    def _():
        for r0 in range(0, d, MXU_DIM):
            w16[r0:r0 + MXU_DIM, :] = w_ref[0, r0:r0 + MXU_DIM, :].astype(BF16)

    acc = res_ref[...]
    for c0 in range(0, d, MXU_DIM):
        parts = []
        for h0 in range(c0, c0 + MXU_DIM, d_head):
            sl = slice(h0, h0 + d_head)
            y = yf_ref[:, sl].astype(F32) + yb_ref[:, sl].astype(F32)
            ms = jnp.mean(y * y, axis=-1, keepdims=True)
            gate = gate_ref[:, sl].astype(F32)
            act = _sigmoid(gate)
            if silu:
                act = gate * act
            parts.append((y * lax.rsqrt(ms + EPS) * hg_ref[:, sl] * act).astype(BF16))
        z = parts[0] if len(parts) == 1 else jnp.concatenate(parts, axis=1)
        acc = acc + jnp.dot(z, w16[c0:c0 + MXU_DIM, :], preferred_element_type=F32)
    out_ref[...] = acc
    hn_ref[:, :d] = _rms_rows(acc, ng_ref[...]).astype(hn_ref.dtype)
    hn_ref[:, d:] = jnp.zeros((hn_ref.shape[0], hn_ref.shape[1] - d), hn_ref.dtype)


def mix_out(yf, yb, p, gate_col_block, head_g, res, w_out, layer, norm_g, d_head, silu, tm=256):
    t, d = yf.shape
    row = lambda i: (i, 0)
    const = lambda i: (0, 0)
    return pl.pallas_call(
        functools.partial(_mix_out_kernel, d_head=d_head, silu=silu),
        grid=(t // tm,),
        in_specs=[
            pl.BlockSpec((tm, d), row),
            pl.BlockSpec((tm, d), row),
            pl.BlockSpec((tm, d), lambda i: (i, gate_col_block)),
            pl.BlockSpec((1, d), const),
            pl.BlockSpec((tm, d), row),
            pl.BlockSpec((1, d, d), lambda i: (layer, 0, 0), pipeline_mode=pl.Buffered(1)),
            pl.BlockSpec((1, d), const),
        ],
        out_specs=[pl.BlockSpec((tm, d), row), pl.BlockSpec((tm, d + HN_PAD), row)],
        out_shape=[jax.ShapeDtypeStruct((t, d), F32), jax.ShapeDtypeStruct((t, d + HN_PAD), BF16)],
        scratch_shapes=[pltpu.VMEM((d, d), BF16)],
        compiler_params=_params(("arbitrary",)),
        name="mix_out",
    )(yf, yb, p, head_g.reshape(1, d), res, w_out, norm_g.reshape(1, d))


def _glu_up_kernel(hn_ref, wa0_ref, wa1_ref, wv0_ref, wv1_ref, cw_ref, cb_ref, g_ref, a_scr, v_scr, w16, *,
                   row_chunk):
    s = hn_ref.shape[0]
    d = w16.shape[0]
    tf = a_scr.shape[1]
    pad = SUBLANES
    zeros = jnp.zeros((pad, tf), F32)
    a_scr[0:pad, :] = zeros
    a_scr[pad + s:2 * pad + s, :] = zeros
    for i, w_ref in enumerate((wa0_ref, wa1_ref, wv0_ref, wv1_ref)):
        w16[:, i * LANES:(i + 1) * LANES] = w_ref[0].astype(BF16)

    def epilogue(first, last):
        n = last - first
        ext = a_scr[first:first + n + 2 * pad, :]
        prev = pltpu.roll(ext, 1, 0)[pad:pad + n, :]
        nxt = pltpu.roll(ext, n + 2 * pad - 1, 0)[pad:pad + n, :]
        c = (cb_ref[...] + prev * cw_ref[0:1, :] + ext[pad:pad + n, :] * cw_ref[1:2, :]
             + nxt * cw_ref[2:3, :])
        gelu = 0.5 * c * (1.0 + lax.erf(c * INV_SQRT2))
        g_ref[first:last, :] = (gelu * v_scr[first:last, :]).astype(BF16)

    trail = 2 * SUBLANES
    bounds = [0]
    while s - bounds[-1] > row_chunk:
        bounds.append(bounds[-1] + (s - bounds[-1]) // 2)
    bounds.append(s)
    done = 0
    for lo, hi in zip(bounds[:-1], bounds[1:]):
        if lo > 0:
            epilogue(done, lo - trail)
            done = lo - trail
        u = jnp.dot(hn_ref[lo:hi, :d], w16[...], preferred_element_type=F32)
        a_scr[pad + lo:pad + hi, :] = u[:, :tf]
        v_scr[lo:hi, :] = u[:, tf:]
    epilogue(done, s)


def glu_up(hn, w_up, layer, conv_w, conv_b, batch, seq):
    t = hn.shape[0]
    d = w_up.shape[1]
    nf = conv_w.shape[1]
    tf = FF_TILE
    nb = D_FF // LANES
    last = 2 * nb - 1
    wspec = lambda off: pl.BlockSpec((1, d, LANES), lambda b, j: (layer, 0, jnp.minimum(off + 2 * j, last)))
    return pl.pallas_call(
        functools.partial(_glu_up_kernel, row_chunk=256),
        grid=(batch, nf // tf),
        in_specs=[
            pl.BlockSpec((seq, hn.shape[1]), lambda b, j: (b, 0)),
            wspec(0), wspec(1), wspec(nb), wspec(nb + 1),
            pl.BlockSpec((3, tf), lambda b, j: (0, j)),
            pl.BlockSpec((1, tf), lambda b, j: (0, j)),
        ],
        out_specs=pl.BlockSpec((seq, tf), lambda b, j: (b, j)),
        out_shape=jax.ShapeDtypeStruct((t, D_FF), BF16),
        scratch_shapes=[pltpu.VMEM((seq + 2 * SUBLANES, tf), F32), pltpu.VMEM((seq, tf), F32),
                        pltpu.VMEM((d, 2 * tf), BF16)],
        compiler_params=_params(("parallel", "arbitrary")),
        name="glu_up",
    )(hn, w_up, w_up, w_up, w_up, conv_w, conv_b)


def _proj_res_kernel(x_ref, w_ref, res_ref, out_ref):
    out_ref[...] = res_ref[...] + jnp.dot(x_ref[...], w_ref[0], preferred_element_type=F32)


def proj_res(x, w, layer, res, tm=1024, tn=512):
    t, k = x.shape
    n = w.shape[2]
    return pl.pallas_call(
        _proj_res_kernel,
        grid=(t // tm, n // tn),
        in_specs=[
            pl.BlockSpec((tm, k), lambda i, j: (i, 0)),
            pl.BlockSpec((1, k, tn), lambda i, j: (layer, 0, j)),
            pl.BlockSpec((tm, tn), lambda i, j: (i, j)),
        ],
        out_specs=pl.BlockSpec((tm, tn), lambda i, j: (i, j)),
        out_shape=jax.ShapeDtypeStruct((t, n), F32),
        compiler_params=_params(("parallel", "arbitrary")),
        name="proj_res",
    )(x, w, res)


def conv_glu(h, hn, w_up, conv_w, conv_b, w_down16, layer, batch, seq):
    fpad = D_FF_PAD - D_FF
    cw = jnp.pad(conv_w[layer], ((0, 0), (0, fpad)))
    cb = jnp.pad(conv_b[layer], (0, fpad)).reshape(1, D_FF_PAD)
    g = glu_up(hn, w_up, layer, cw, cb, batch, seq)
    return proj_res(g, w_down16, layer, h)


def kernel(x, norm_mix_g, norm_ffn_g, mlstm_w_in, mlstm_b_gate, mlstm_head_g, mlstm_w_out,
           hgrn_w_in, hgrn_lb, hgrn_head_g, hgrn_w_out, ffn_w_up, ffn_conv_w, ffn_conv_b,
           ffn_w_down, final_g):
    batch, seq, d = x.shape
    h = x.reshape(batch * seq, d)
    depth = norm_mix_g.shape[0]
    w_down16 = ffn_w_down.astype(BF16)
    for layer in range(depth):
        j = layer // 2
        if layer % 2 == 0:
            n_main = 2 * A_HEADS * A_DK + 2 * A_HEADS * A_DV
            n_gate = mlstm_w_in.shape[2] - n_main
            w_gate = jnp.pad(mlstm_w_in[j, :, n_main:], ((0, 0), (0, LANES - n_gate))).astype(BF16)
            p, gates = norm_proj(h, norm_mix_g[layer], mlstm_w_in, j, n_main, BF16, w_gate=w_gate)
            bias = jnp.pad(mlstm_b_gate[j], (0, LANES - n_gate))
            yf, yb = mlstm_scan(p, gates, bias, batch, seq)
            h, hn = mix_out(yf, yb, p, 2, mlstm_head_g[j], h, mlstm_w_out, j, norm_ffn_g[layer], A_DV,
                            silu=False)
        else:
            p = norm_proj(h, norm_mix_g[layer], hgrn_w_in, j, hgrn_w_in.shape[2], BF16)
            yf, yb = hgrn_scan(p, hgrn_lb, layer, batch, seq)
            h, hn = mix_out(yf, yb, p, 2, hgrn_head_g[j], h, hgrn_w_out, j, norm_ffn_g[layer], B_DV,
                            silu=True)
        h = conv_glu(h, hn, ffn_w_up, ffn_conv_w, ffn_conv_b, w_down16, layer, batch, seq)
    return rmsnorm(h, final_g, x.dtype).reshape(batch, seq, d)
```

```python
import functools
import math

import numpy as np
import jax
import jax.numpy as jnp
from jax import lax
from jax.experimental import pallas as pl
from jax.experimental.pallas import tpu as pltpu

F32 = jnp.float32
BF16 = jnp.bfloat16

EPS = 1e-6
D_MODEL = 2048
LANES = 128
SUBLANES = 8
MXU_DIM = 256
VMEM_LIMIT_BYTES = 56 * 1024 * 1024

A_HEADS, A_DK, A_DV = 8, 128, 256
A_CHUNK = 256
A_HEADS_PER_STEP = 4
B_HEADS, B_DK, B_DV = 16, 128, 128
B_CHUNK = 128
B_LEVELS = 7
B_HEADS_PER_STEP = 4
D_FF = 5504
FF_TILE = 256
D_FF_PAD = 5632
HN_PAD = LANES
INV_SQRT2 = 0.7071067811865476
LOG2E = 1.4426950408889634
MASKED = -1e30

_NT = (((1,), (1,)), ((), ()))


def _params(sem):
    return pltpu.CompilerParams(dimension_semantics=sem, vmem_limit_bytes=VMEM_LIMIT_BYTES)


def _rms_rows(x, g):
    ms = jnp.mean(x * x, axis=-1, keepdims=True)
    return x * lax.rsqrt(ms + EPS) * g


def _sigmoid(x):
    return 1.0 / (1.0 + jnp.exp(-x))


def _log_sigmoid(x):
    return jnp.minimum(x, 0.0) - jnp.log1p(jnp.exp(-jnp.abs(x)))


def _norm_proj_kernel(x_ref, g_ref, w_ref, *rest, row_chunk, has_gate, w_is_transposed):
    if has_gate:
        wg_ref, o_ref, og_ref, hn_ref = rest
    else:
        o_ref, hn_ref = rest
    contract = _NT if w_is_transposed else (((1,), (0,)), ((), ()))

    @pl.when(pl.program_id(1) == 0)
    def _():
        def body(r, carry):
            rows = pl.ds(pl.multiple_of(r * row_chunk, row_chunk), row_chunk)
            hn_ref[rows, :] = _rms_rows(x_ref[rows, :], g_ref[...]).astype(BF16)
            return carry

        lax.fori_loop(0, x_ref.shape[0] // row_chunk, body, 0)
        if has_gate:
            og_ref[...] = lax.dot_general(hn_ref[...], wg_ref[...], contract, preferred_element_type=F32)

    o_ref[...] = lax.dot_general(hn_ref[...], w_ref[0].astype(BF16), contract,
                                 preferred_element_type=F32).astype(o_ref.dtype)


def norm_proj(x, g, w, layer, n, out_dtype, w_gate=None, w_is_transposed=False, tm=2048, tn=512):
    t, d = x.shape
    has_gate = w_gate is not None
    in_specs = [
        pl.BlockSpec((tm, d), lambda i, j: (i, 0), pipeline_mode=pl.Buffered(1)),
        pl.BlockSpec((1, d), lambda i, j: (0, 0)),
        (pl.BlockSpec((1, tn, d), lambda i, j: (layer, j, 0)) if w_is_transposed
         else pl.BlockSpec((1, d, tn), lambda i, j: (layer, 0, j))),
    ]
    out_specs = [pl.BlockSpec((tm, tn), lambda i, j: (i, j))]
    out_shape = [jax.ShapeDtypeStruct((t, n), out_dtype)]
    args = [x, g.reshape(1, d), w]
    if has_gate:
        ng = w_gate.shape[0 if w_is_transposed else 1]
        in_specs.append(pl.BlockSpec(w_gate.shape, lambda i, j: (0, 0)))
        out_specs.append(pl.BlockSpec((tm, ng), lambda i, j: (i, 0)))
        out_shape.append(jax.ShapeDtypeStruct((t, ng), F32))
        args.append(w_gate)
    outs = pl.pallas_call(
        functools.partial(_norm_proj_kernel, row_chunk=64, has_gate=has_gate, w_is_transposed=w_is_transposed),
        grid=(t // tm, n // tn),
        in_specs=in_specs,
        out_specs=out_specs,
        out_shape=out_shape,
        scratch_shapes=[pltpu.VMEM((tm, d), BF16)],
        compiler_params=_params(("parallel", "arbitrary")),
        name="norm_proj",
    )(*args)
    return outs if has_gate else outs[0]


class _Stream:
    def __init__(self, **kw):
        self.__dict__.update(kw)


def _sublane_scan(x, op, identity, reverse):
    L = x.shape[0]
    row = lax.broadcasted_iota(jnp.int32, x.shape, 0)
    k = 1
    while k < L:
        if reverse:
            shifted, valid = pltpu.roll(x, L - k, 0), row < L - k
        else:
            shifted, valid = pltpu.roll(x, k, 0), row >= k
        x = op(x, jnp.where(valid, shifted, identity))
        k *= 2
    return x


def _mlstm_gates(ig, f_pre, m_prev, reverse):
    L = ig.shape[0]
    ln_scale = -0.5 * math.log(A_DK)
    last = slice(0, 1) if reverse else slice(L - 1, L)
    b = _sublane_scan(_log_sigmoid(f_pre), jnp.add, 0.0, reverse)
    u = ig - b
    cm = _sublane_scan(u, jnp.maximum, -jnp.inf, reverse)
    mx = jnp.maximum(cm, m_prev)
    mn = jnp.maximum(cm[last, :], m_prev)
    u2 = u * LOG2E
    g = _Stream(
        cv2=(ln_scale - mx) * LOG2E,
        rv2t=jnp.concatenate([u2[i * LANES:(i + 1) * LANES, :].T for i in range(L // LANES)], axis=1),
        w_inter=jnp.exp(m_prev - mx + ln_scale),
        floor=jnp.exp(-mx - b),
        w_s=jnp.exp(u - mn),
        decay=jnp.exp(m_prev - mn),
        m_new=b[last, :] + mn)
    return g


def _mlstm_chunk(streams):
    for s in streams:
        qk = lax.dot_general(s.q, s.k, _NT, preferred_element_type=F32)
        s.p = qk * jnp.exp2(s.cv2 + s.rv2 + s.negmask)
    for s in streams:
        num = (jnp.dot(s.p.astype(BF16), s.v, preferred_element_type=F32)
               + s.w_inter * jnp.dot(s.q, s.ct_ref[...].astype(BF16), preferred_element_type=F32))
        den = (jnp.sum(s.p, axis=-1, keepdims=True)
               + s.w_inter * jnp.sum(s.q.astype(F32) * s.n_ref[...], axis=-1, keepdims=True))
        s.h = num * (1.0 / jnp.maximum(jnp.abs(den), s.floor))
    for s in streams:
        kw = s.k.astype(F32) * s.w_s
        s.ct_ref[...] = s.decay * s.ct_ref[...] + jnp.dot(kw.T.astype(BF16), s.v, preferred_element_type=F32)
        s.n_ref[...] = s.decay * s.n_ref[...] + jnp.sum(kw, axis=0, keepdims=True)


def _mlstm_kernel(bi_ref, bf_ref, qf_ref, kf_ref, vf_ref, gif_ref, gff_ref, qb_ref, kb_ref, vb_ref, gib_ref,
                  gfb_ref, negf_ref, negb_ref, yf_ref, yb_ref, ctf, nf, mf, ctb, nb, mb, *, heads):
    @pl.when(pl.program_id(2) == 0)
    def _():
        for ref in (ctf, nf, mf, ctb, nb, mb):
            ref[...] = jnp.zeros_like(ref)

    gates_f = _mlstm_gates(gif_ref[0, 0] + bi_ref[0], gff_ref[0, 0] + bf_ref[0], mf[...], False)
    gates_b = _mlstm_gates(gib_ref[0, 0] + bi_ref[0], gfb_ref[0, 0] + bf_ref[0], mb[...], True)
    mf[...] = gates_f.m_new
    mb[...] = gates_b.m_new

    streams = []
    for hh in range(heads):
        ks = slice(hh * A_DK, (hh + 1) * A_DK)
        vs = slice(hh * A_DV, (hh + 1) * A_DV)
        for g, lane, q_ref, k_ref, v_ref, neg_ref, ct, n in (
                (gates_f, hh, qf_ref, kf_ref, vf_ref, negf_ref, ctf, nf),
                (gates_b, heads + hh, qb_ref, kb_ref, vb_ref, negb_ref, ctb, nb)):
            col = slice(lane, lane + 1)
            streams.append(_Stream(
                q=q_ref[:, ks].astype(BF16), k=k_ref[:, ks].astype(BF16), v=v_ref[:, vs].astype(BF16),
                cv2=g.cv2[:, col], rv2=g.rv2t[col, :], w_inter=g.w_inter[:, col], floor=g.floor[:, col],
                w_s=g.w_s[:, col], decay=g.decay[:, col], negmask=neg_ref[...],
                ct_ref=ct.at[hh], n_ref=n.at[hh]))
    _mlstm_chunk(streams)
    for hh in range(heads):
        vs = slice(hh * A_DV, (hh + 1) * A_DV)
        yf_ref[:, vs] = streams[2 * hh].h.astype(yf_ref.dtype)
        yb_ref[:, vs] = streams[2 * hh + 1].h.astype(yb_ref.dtype)


def _group_gate_lanes(x, batch, seq, heads):
    ng = A_HEADS // heads
    x = x[:, :4 * A_HEADS].reshape(batch, seq, 2, 2, ng, heads)
    x = jnp.transpose(x, (3, 0, 4, 1, 2, 5)).reshape(2, batch, ng, seq, 2 * heads)
    x = jnp.pad(x, ((0, 0),) * 4 + ((0, LANES - 2 * heads),))
    return x[0], x[1]


def mlstm_scan(p, gates, bias, batch, seq, heads=A_HEADS_PER_STEP):
    L = A_CHUNK
    nc = seq // L
    t = batch * seq
    ng = A_HEADS // heads
    g_in, g_forget = _group_gate_lanes(gates, batch, seq, heads)
    b_in, b_forget = _group_gate_lanes(bias.reshape(1, -1), 1, 1, heads)
    keep = np.tril(np.ones((L, L), np.float32))
    neg_f = jnp.asarray((1.0 - keep) * MASKED)
    neg_b = jnp.asarray((1.0 - keep.T) * MASKED)

    fwd = lambda b, h, c: b * nc + c
    bwd = lambda b, h, c: b * nc + (nc - 1 - c)

    def dir_specs(row):
        gate = pl.BlockSpec((1, 1, L, LANES), lambda b, h, c: (b, h, row(b, h, c) - b * nc, 0))
        return [
            pl.BlockSpec((L, heads * A_DK), lambda b, h, c: (row(b, h, c), h)),
            pl.BlockSpec((L, heads * A_DK), lambda b, h, c: (row(b, h, c), ng + h)),
            pl.BlockSpec((L, heads * A_DV), lambda b, h, c: (row(b, h, c), ng + h)),
            gate, gate,
        ]

    const = pl.BlockSpec((L, L), lambda b, h, c: (0, 0))
    bias_spec = pl.BlockSpec((1, 1, LANES), lambda b, h, c: (h, 0, 0))
    in_specs = [bias_spec, bias_spec] + dir_specs(fwd) + dir_specs(bwd) + [const, const]
    out_specs = [
        pl.BlockSpec((L, heads * A_DV), lambda b, h, c: (fwd(b, h, c), h)),
        pl.BlockSpec((L, heads * A_DV), lambda b, h, c: (bwd(b, h, c), h)),
    ]
    state = [pltpu.VMEM((heads, A_DK, A_DV), F32), pltpu.VMEM((heads, 1, A_DK), F32),
             pltpu.VMEM((1, LANES), F32)]
    return pl.pallas_call(
        functools.partial(_mlstm_kernel, heads=heads),
        grid=(batch, ng, nc),
        in_specs=in_specs,
        out_specs=out_specs,
        out_shape=[jax.ShapeDtypeStruct((t, A_HEADS * A_DV), BF16)] * 2,
        scratch_shapes=state + state,
        compiler_params=_params(("parallel", "parallel", "arbitrary")),
        name="mlstm_scan",
    )(b_in[0], b_forget[0], p, p, p, g_in, g_forget, p, p, p, g_in, g_forget, neg_f, neg_b)


def _mid_rows(b, c, r_off):
    L = b.shape[0]
    if c >= SUBLANES:
        b3 = b.reshape(L // c, c, LANES)
        return jnp.broadcast_to(b3[:, r_off:r_off + 1, :], b3.shape).reshape(L, LANES)
    b3 = b.reshape(L // SUBLANES, SUBLANES, LANES)
    lo = jnp.broadcast_to(b3[:, r_off:r_off + 1, :], b3.shape)
    hi = jnp.broadcast_to(b3[:, c + r_off:c + r_off + 1, :], b3.shape)
    sub = lax.broadcasted_iota(jnp.int32, b3.shape, 1)
    return jnp.where(sub < c, lo, hi).reshape(L, LANES)


def _midpoint_exponent(b, half, reverse):
    L = b.shape[0]
    pieces = []
    for lo in range(0, L, 2 * half):
        first, second = b[lo:lo + half, :], b[lo + half:lo + 2 * half, :]
        if reverse:
            mid = b[lo + half:lo + half + 1, :]
            pieces += [first - mid, mid - second]
        else:
            mid = b[lo + half - 1:lo + half, :]
            pieces += [mid - first, second - mid]
    return jnp.concatenate(pieces, axis=0)


def _split3(x):
    hi = x.astype(BF16)
    r1 = x - hi.astype(F32)
    mid = r1.astype(BF16)
    lo = (r1 - mid.astype(F32)).astype(BF16)
    return hi, mid, lo


def _hgrn_chunk(streams):
    L = streams[0].q.shape[0]
    row = lax.broadcasted_iota(jnp.int32, (L, LANES), 0)
    odd = (row & 1) == 1

    for s in streams:
        e = jnp.exp(-jnp.abs(s.f))
        r = 1.0 / (1.0 + e)
        er = e * r
        pos = s.f >= 0.0
        s.forget = s.lb + (1.0 - s.lb) * jnp.where(pos, r, er)
        s.kk = (1.0 - s.lb) * jnp.where(pos, er, r)
    for s in streams:
        s.b = jnp.dot(s.tri3, jnp.concatenate(_split3(jnp.log2(s.forget)), axis=0),
                      preferred_element_type=F32)
        s.b_tot = s.b[0:1, :] if s.reverse else s.b[L - 1:L, :]
    for s in streams:
        s.st = s.st_ref[...]
        s.k16 = s.kk.astype(BF16)
        s.o = lax.dot_general(s.q * jnp.exp2(s.b).astype(BF16), s.st.astype(BF16), _NT,
                              preferred_element_type=F32)
        s.attn = None

    for j in range(-1, B_LEVELS):
        half = 1 << max(j, 0)
        for s in streams:
            if j < 0:
                lhs, rhs = s.q, s.k16
            elif j == 0:
                w16 = jnp.where(odd != s.reverse, s.forget, 1.0).astype(BF16)
                lhs, rhs = s.q * w16, s.k16 * w16
            else:
                if half < SUBLANES:
                    b_mid = _mid_rows(s.b, 2 * half, half if s.reverse else half - 1)
                    w16 = jnp.exp2(-jnp.abs(s.b - b_mid)).astype(BF16)
                else:
                    w16 = jnp.exp2(_midpoint_exponent(s.b, half, s.reverse)).astype(BF16)
                lhs, rhs = s.q * w16, s.k16 * w16
            pj = (lax.dot_general(lhs, rhs, _NT, preferred_element_type=F32).astype(BF16)
                  * s.masks_ref[j % (B_LEVELS + 1)])
            s.attn = pj if s.attn is None else s.attn + pj
    for s in streams:
        s.o = s.o + jnp.dot(s.attn, s.v, preferred_element_type=F32)
    for s in streams:
        kt = s.k16 * jnp.exp2(s.b_tot - s.b).astype(BF16)
        s.st_ref[...] = jnp.exp2(s.b_tot) * s.st + jnp.dot(s.v.astype(F32).T.astype(BF16), kt,
                                                           preferred_element_type=F32)


def _hgrn_kernel(lbp_ref, qf_ref, vf_ref, ff_ref, qb_ref, vb_ref, fb_ref,
                 trif_ref, trib_ref, maskf_ref, maskb_ref, yf_ref, yb_ref, stf, stb, *, layer, heads):
    @pl.when(pl.program_id(2) == 0)
    def _():
        stf[...] = jnp.zeros_like(stf)
        stb[...] = jnp.zeros_like(stb)

    lbp = lbp_ref[...]
    ex = jnp.exp(lbp - jnp.max(lbp, axis=0, keepdims=True))
    sm = ex / jnp.sum(ex, axis=0, keepdims=True)
    lb = jnp.sum(sm[0:layer + 1, :], axis=0, keepdims=True) - sm[0:1, :]

    tri_f, tri_b = trif_ref[...], trib_ref[...]
    streams = []
    for hh in range(heads):
        ks = slice(hh * B_DK, (hh + 1) * B_DK)
        vs = slice(hh * B_DV, (hh + 1) * B_DV)
        streams.append(_Stream(q=qf_ref[:, ks], v=vf_ref[:, vs], f=ff_ref[:, ks].astype(F32), lb=lb[:, ks],
                               tri3=tri_f, masks_ref=maskf_ref, st_ref=stf.at[hh], reverse=False))
        streams.append(_Stream(q=qb_ref[:, ks], v=vb_ref[:, vs], f=fb_ref[:, ks].astype(F32), lb=lb[:, ks],
                               tri3=tri_b, masks_ref=maskb_ref, st_ref=stb.at[hh], reverse=True))
    _hgrn_chunk(streams)
    for hh in range(heads):
        vs = slice(hh * B_DV, (hh + 1) * B_DV)
        yf_ref[:, vs] = streams[2 * hh].o.astype(yf_ref.dtype)
        yb_ref[:, vs] = streams[2 * hh + 1].o.astype(yb_ref.dtype)


def _hgrn_masks(L):
    t = np.arange(L)[:, None]
    s = np.arange(L)[None, :]
    out = []
    for j in range(B_LEVELS):
        half = 1 << j
        c = 2 * half
        out.append(((t // c == s // c) & (t % c >= half) & (s % c < half)).astype(np.float32))
    out.append(np.eye(L, dtype=np.float32))
    return np.stack(out)


def hgrn_scan(p, lb_param, layer, batch, seq, heads=B_HEADS_PER_STEP):
    L = B_CHUNK
    nc = seq // L
    t = batch * seq
    ng = B_HEADS // heads
    tri_f = jnp.asarray(np.tile(np.tril(np.ones((L, L), np.float32)), (1, 3)), dtype=BF16)
    tri_b = jnp.asarray(np.tile(np.triu(np.ones((L, L), np.float32)), (1, 3)), dtype=BF16)
    masks_f = _hgrn_masks(L)
    masks_b = np.ascontiguousarray(np.transpose(masks_f, (0, 2, 1)))
    depth = lb_param.shape[0]

    fwd = lambda b, h, c: b * nc + c
    bwd = lambda b, h, c: b * nc + (nc - 1 - c)

    def dir_specs(row, fcol):
        return [
            pl.BlockSpec((L, heads * B_DK), lambda b, h, c: (row(b, h, c), h)),
            pl.BlockSpec((L, heads * B_DV), lambda b, h, c: (row(b, h, c), ng + h)),
            pl.BlockSpec((L, heads * B_DK), lambda b, h, c: (row(b, h, c), fcol * ng + h)),
        ]

    const2 = pl.BlockSpec((L, 3 * L), lambda b, h, c: (0, 0))
    const3 = pl.BlockSpec((B_LEVELS + 1, L, L), lambda b, h, c: (0, 0, 0))
    in_specs = ([pl.BlockSpec((depth, heads * B_DK), lambda b, h, c: (0, h))]
                + dir_specs(fwd, 3) + dir_specs(bwd, 4) + [const2, const2, const3, const3])
    out_specs = [
        pl.BlockSpec((L, heads * B_DV), lambda b, h, c: (fwd(b, h, c), h)),
        pl.BlockSpec((L, heads * B_DV), lambda b, h, c: (bwd(b, h, c), h)),
    ]
    return pl.pallas_call(
        functools.partial(_hgrn_kernel, layer=layer, heads=heads),
        grid=(batch, ng, nc),
        in_specs=in_specs,
        out_specs=out_specs,
        out_shape=[jax.ShapeDtypeStruct((t, B_HEADS * B_DV), BF16)] * 2,
        scratch_shapes=[pltpu.VMEM((heads, B_DV, B_DK), F32)] * 2,
        compiler_params=_params(("parallel", "parallel", "arbitrary")),
        name="hgrn_scan",
    )(lb_param, p, p, p, p, p, p, tri_f, tri_b, jnp.asarray(masks_f, dtype=BF16), jnp.asarray(masks_b, dtype=BF16))


def _mix_out_kernel(yf_ref, yb_ref, gate_ref, hg_ref, res_ref, w_ref, ng_ref, out_ref, hn_ref, w16, *, d_head, silu):
    d = res_ref.shape[1]

    @pl.when(pl.program_id(0) == 0)
    def _():
        for r0 in range(0, d, MXU_DIM):
            w16[r0:r0 + MXU_DIM, :] = w_ref[0, r0:r0 + MXU_DIM, :].astype(BF16)

    acc = res_ref[...]
    for c0 in range(0, d, MXU_DIM):
        parts = []
        for h0 in range(c0, c0 + MXU_DIM, d_head):
            sl = slice(h0, h0 + d_head)
            y = yf_ref[:, sl].astype(F32) + yb_ref[:, sl].astype(F32)
            ms = jnp.mean(y * y, axis=-1, keepdims=True)
            gate = gate_ref[:, sl].astype(F32)
            act = _sigmoid(gate)
            if silu:
                act = gate * act
            parts.append((y * lax.rsqrt(ms + EPS) * hg_ref[:, sl] * act).astype(BF16))
        z = parts[0] if len(parts) == 1 else jnp.concatenate(parts, axis=1)
        acc = acc + jnp.dot(z, w16[c0:c0 + MXU_DIM, :], preferred_element_type=F32)
    out_ref[...] = acc
    hn_ref[:, :d] = _rms_rows(acc, ng_ref[...]).astype(hn_ref.dtype)
    hn_ref[:, d:] = jnp.zeros((hn_ref.shape[0], hn_ref.shape[1] - d), hn_ref.dtype)


def mix_out(yf, yb, p, gate_col_block, head_g, res, w_out, layer, norm_g, d_head, silu, tm=256):
    t, d = yf.shape
    row = lambda i: (i, 0)
    const = lambda i: (0, 0)
    return pl.pallas_call(
        functools.partial(_mix_out_kernel, d_head=d_head, silu=silu),
        grid=(t // tm,),
        in_specs=[
            pl.BlockSpec((tm, d), row),
            pl.BlockSpec((tm, d), row),
            pl.BlockSpec((tm, d), lambda i: (i, gate_col_block)),
            pl.BlockSpec((1, d), const),
            pl.BlockSpec((tm, d), row),
            pl.BlockSpec((1, d, d), lambda i: (layer, 0, 0), pipeline_mode=pl.Buffered(1)),
            pl.BlockSpec((1, d), const),
        ],
        out_specs=[pl.BlockSpec((tm, d), row), pl.BlockSpec((tm, d + HN_PAD), row)],
        out_shape=[jax.ShapeDtypeStruct((t, d), F32), jax.ShapeDtypeStruct((t, d + HN_PAD), BF16)],
        scratch_shapes=[pltpu.VMEM((d, d), BF16)],
        compiler_params=_params(("arbitrary",)),
        name="mix_out",
    )(yf, yb, p, head_g.reshape(1, d), res, w_out, norm_g.reshape(1, d))


def _glu_up_kernel(hn_ref, wa0_ref, wa1_ref, wv0_ref, wv1_ref, cw_ref, cb_ref, g_ref, a_scr, v_scr, w16, *,
                   row_chunk):
    s = hn_ref.shape[0]
    d = w16.shape[0]
    tf = a_scr.shape[1]
    pad = SUBLANES
    zeros = jnp.zeros((pad, tf), F32)
    a_scr[0:pad, :] = zeros
    a_scr[pad + s:2 * pad + s, :] = zeros
    for i, w_ref in enumerate((wa0_ref, wa1_ref, wv0_ref, wv1_ref)):
        w16[:, i * LANES:(i + 1) * LANES] = w_ref[0].astype(BF16)

    def epilogue(first, last):
        n = last - first
        ext = a_scr[first:first + n + 2 * pad, :]
        prev = pltpu.roll(ext, 1, 0)[pad:pad + n, :]
        nxt = pltpu.roll(ext, n + 2 * pad - 1, 0)[pad:pad + n, :]
        c = (cb_ref[...] + prev * cw_ref[0:1, :] + ext[pad:pad + n, :] * cw_ref[1:2, :]
             + nxt * cw_ref[2:3, :])
        gelu = 0.5 * c * (1.0 + lax.erf(c * INV_SQRT2))
        g_ref[first:last, :] = (gelu * v_scr[first:last, :]).astype(BF16)

    trail = 2 * SUBLANES
    bounds = [0]
    while s - bounds[-1] > row_chunk:
        bounds.append(bounds[-1] + (s - bounds[-1]) // 2)
    bounds.append(s)
    done = 0
    for lo, hi in zip(bounds[:-1], bounds[1:]):
        if lo > 0:
            epilogue(done, lo - trail)
            done = lo - trail
        u = jnp.dot(hn_ref[lo:hi, :d], w16[...], preferred_element_type=F32)
        a_scr[pad + lo:pad + hi, :] = u[:, :tf]
        v_scr[lo:hi, :] = u[:, tf:]
    epilogue(done, s)


def glu_up(hn, w_up, layer, conv_w, conv_b, batch, seq):
    t = hn.shape[0]
    d = w_up.shape[1]
    nf = conv_w.shape[1]
    tf = FF_TILE
    nb = D_FF // LANES
    last = 2 * nb - 1
    wspec = lambda off: pl.BlockSpec((1, d, LANES), lambda b, j: (layer, 0, jnp.minimum(off + 2 * j, last)))
    return pl.pallas_call(
        functools.partial(_glu_up_kernel, row_chunk=256),
        grid=(batch, nf // tf),
        in_specs=[
            pl.BlockSpec((seq, hn.shape[1]), lambda b, j: (b, 0)),
            wspec(0), wspec(1), wspec(nb), wspec(nb + 1),
            pl.BlockSpec((3, tf), lambda b, j: (0, j)),
            pl.BlockSpec((1, tf), lambda b, j: (0, j)),
        ],
        out_specs=pl.BlockSpec((seq, tf), lambda b, j: (b, j)),
        out_shape=jax.ShapeDtypeStruct((t, D_FF), BF16),
        scratch_shapes=[pltpu.VMEM((seq + 2 * SUBLANES, tf), F32), pltpu.VMEM((seq, tf), F32),
                        pltpu.VMEM((d, 2 * tf), BF16)],
        compiler_params=_params(("parallel", "arbitrary")),
        name="glu_up",
    )(hn, w_up, w_up, w_up, w_up, conv_w, conv_b)


def _proj_res_kernel(x_ref, w_ref, res_ref, out_ref):
    out_ref[...] = res_ref[...] + jnp.dot(x_ref[...], w_ref[0], preferred_element_type=F32)


def proj_res(x, w, layer, res, tm=1024, tn=512):
    t, k = x.shape
    n = w.shape[2]
    return pl.pallas_call(
        _proj_res_kernel,
        grid=(t // tm, n // tn),
        in_specs=[
            pl.BlockSpec((tm, k), lambda i, j: (i, 0)),
            pl.BlockSpec((1, k, tn), lambda i, j: (layer, 0, j)),
            pl.BlockSpec((tm, tn), lambda i, j: (i, j)),
        ],
        out_specs=pl.BlockSpec((tm, tn), lambda i, j: (i, j)),
        out_shape=jax.ShapeDtypeStruct((t, n), F32),
        compiler_params=_params(("parallel", "arbitrary")),
        name="proj_res",
    )(x, w, res)


def _proj_res_norm_kernel(x_ref, w_ref, res_ref, g_ref, *out_refs, keep_sum):
    acc = res_ref[...] + jnp.dot(x_ref[...], w_ref[0], preferred_element_type=F32)
    if keep_sum:
        h_ref, hn_ref = out_refs
        h_ref[...] = acc
    else:
        (hn_ref,) = out_refs
    d = acc.shape[1]
    hn_ref[:, :d] = _rms_rows(acc, g_ref[...]).astype(hn_ref.dtype)
    if hn_ref.shape[1] > d:
        hn_ref[:, d:] = jnp.zeros((hn_ref.shape[0], hn_ref.shape[1] - d), hn_ref.dtype)


def proj_res_norm(x, w, layer, res, g, norm_dtype, norm_pad, keep_sum, tm=512):
    t, k = x.shape
    n = w.shape[2]
    row = lambda i: (i, 0)
    out_specs = [pl.BlockSpec((tm, n + norm_pad), row)]
    out_shape = [jax.ShapeDtypeStruct((t, n + norm_pad), norm_dtype)]
    if keep_sum:
        out_specs.insert(0, pl.BlockSpec((tm, n), row))
        out_shape.insert(0, jax.ShapeDtypeStruct((t, n), F32))
    outs = pl.pallas_call(
        functools.partial(_proj_res_norm_kernel, keep_sum=keep_sum),
        grid=(t // tm,),
        in_specs=[
            pl.BlockSpec((tm, k), row),
            pl.BlockSpec((1, k, n), lambda i: (layer, 0, 0), pipeline_mode=pl.Buffered(1)),
            pl.BlockSpec((tm, n), row),
            pl.BlockSpec((1, n), lambda i: (0, 0)),
        ],
        out_specs=out_specs,
        out_shape=out_shape,
        compiler_params=_params(("parallel",)),
        name="proj_res_norm",
    )(x, w, res, g.reshape(1, n))
    return outs if keep_sum else outs[0]


def _proj_kernel(x_ref, w_ref, o_ref):
    d = w_ref.shape[1]
    o_ref[...] = jnp.dot(x_ref[:, :d], w_ref[0].astype(BF16), preferred_element_type=F32).astype(o_ref.dtype)


def proj(x, w, layer, out_dtype, tm=2048, tn=512):
    t = x.shape[0]
    d, n = w.shape[1], w.shape[2]
    return pl.pallas_call(
        _proj_kernel,
        grid=(t // tm, n // tn),
        in_specs=[
            pl.BlockSpec((tm, x.shape[1]), lambda i, j: (i, 0)),
            pl.BlockSpec((1, d, tn), lambda i, j: (layer, 0, j)),
        ],
        out_specs=pl.BlockSpec((tm, tn), lambda i, j: (i, j)),
        out_shape=jax.ShapeDtypeStruct((t, n), out_dtype),
        compiler_params=_params(("parallel", "arbitrary")),
        name="proj",
    )(x, w)


def conv_glu_up(hn, w_up, conv_w, conv_b, layer, batch, seq):
    fpad = D_FF_PAD - D_FF
    cw = jnp.pad(conv_w[layer], ((0, 0), (0, fpad)))
    cb = jnp.pad(conv_b[layer], (0, fpad)).reshape(1, D_FF_PAD)
    return glu_up(hn, w_up, layer, cw, cb, batch, seq)


def kernel(x, norm_mix_g, norm_ffn_g, mlstm_w_in, mlstm_b_gate, mlstm_head_g, mlstm_w_out,
           hgrn_w_in, hgrn_lb, hgrn_head_g, hgrn_w_out, ffn_w_up, ffn_conv_w, ffn_conv_b,
           ffn_w_down, final_g):
    batch, seq, d = x.shape
    h = x.reshape(batch * seq, d)
    depth = norm_mix_g.shape[0]
    w_down16 = ffn_w_down.astype(BF16)
    hn_mix = None
    for layer in range(depth):
        j = layer // 2
        if layer % 2 == 0:
            n_gates = 4 * A_HEADS
            n_main = mlstm_w_in.shape[2] - n_gates
            w_in_t = jnp.swapaxes(mlstm_w_in, 1, 2)
            w_gate = w_in_t[j, -LANES:, :].astype(BF16)
            p, gates = norm_proj(h, norm_mix_g[layer], w_in_t, j, n_main, BF16, w_gate=w_gate,
                                 w_is_transposed=True)
            yf, yb = mlstm_scan(p, gates[:, LANES - n_gates:], mlstm_b_gate[j], batch, seq)
            h, hn = mix_out(yf, yb, p, 2, mlstm_head_g[j], h, mlstm_w_out, j, norm_ffn_g[layer], A_DV,
                            silu=False)
        else:
            if hn_mix is None:
                p = norm_proj(h, norm_mix_g[layer], hgrn_w_in, j, hgrn_w_in.shape[2], BF16)
            else:
                p = proj(hn_mix, hgrn_w_in, j, BF16)
            yf, yb = hgrn_scan(p, hgrn_lb, layer, batch, seq)
            h, hn = mix_out(yf, yb, p, 2, hgrn_head_g[j], h, hgrn_w_out, j, norm_ffn_g[layer], B_DV,
                            silu=True)
        g = conv_glu_up(hn, ffn_w_up, ffn_conv_w, ffn_conv_b, layer, batch, seq)
        if layer == depth - 1:
            return proj_res_norm(g, w_down16, layer, h, final_g, x.dtype, 0, keep_sum=False).reshape(batch, seq, d)
        if (layer + 1) % 2 == 1:
            h, hn_mix = proj_res_norm(g, w_down16, layer, h, norm_mix_g[layer + 1], BF16, HN_PAD, keep_sum=True)
        else:
            h, hn_mix = proj_res(g, w_down16, layer, h), None
```

```python
import functools
import math

import numpy as np
import jax
import jax.numpy as jnp
from jax import lax
from jax.experimental import pallas as pl
from jax.experimental.pallas import tpu as pltpu

F32 = jnp.float32
BF16 = jnp.bfloat16

EPS = 1e-6
D_MODEL = 2048
LANES = 128
SUBLANES = 8
MXU_DIM = 256
VMEM_LIMIT_BYTES = 56 * 1024 * 1024

A_HEADS, A_DK, A_DV = 8, 128, 256
A_CHUNK = 256
A_HEADS_PER_STEP = 4
B_HEADS, B_DK, B_DV = 16, 128, 128
B_CHUNK = 128
B_LEVELS = 7
B_HEADS_PER_STEP = 2
D_FF = 5504
FF_TILE = 256
D_FF_PAD = 5632
HN_PAD = LANES
INV_SQRT2 = 0.7071067811865476
LOG2E = 1.4426950408889634
MASKED = -1e30

_NT = (((1,), (1,)), ((), ()))


def _params(sem):
    return pltpu.CompilerParams(dimension_semantics=sem, vmem_limit_bytes=VMEM_LIMIT_BYTES)


def _rms_rows(x, g):
    ms = jnp.mean(x * x, axis=-1, keepdims=True)
    return x * lax.rsqrt(ms + EPS) * g


def _sigmoid(x):
    return 1.0 / (1.0 + jnp.exp(-x))


def _log_sigmoid(x):
    return jnp.minimum(x, 0.0) - jnp.log1p(jnp.exp(-jnp.abs(x)))


def _norm_proj_kernel(x_ref, g_ref, w_ref, *rest, row_chunk, has_gate, w_is_transposed):
    if has_gate:
        wg_ref, o_ref, og_ref, hn_ref = rest
    else:
        o_ref, hn_ref = rest
    contract = _NT if w_is_transposed else (((1,), (0,)), ((), ()))

    @pl.when(pl.program_id(1) == 0)
    def _():
        def body(r, carry):
            rows = pl.ds(pl.multiple_of(r * row_chunk, row_chunk), row_chunk)
            hn_ref[rows, :] = _rms_rows(x_ref[rows, :], g_ref[...]).astype(BF16)
            return carry

        lax.fori_loop(0, x_ref.shape[0] // row_chunk, body, 0)
        if has_gate:
            og_ref[...] = lax.dot_general(hn_ref[...], wg_ref[...], contract, preferred_element_type=F32)

    o_ref[...] = lax.dot_general(hn_ref[...], w_ref[0].astype(BF16), contract,
                                 preferred_element_type=F32).astype(o_ref.dtype)


def norm_proj(x, g, w, layer, n, out_dtype, w_gate=None, w_is_transposed=False, tm=2048, tn=512):
    t, d = x.shape
    has_gate = w_gate is not None
    in_specs = [
        pl.BlockSpec((tm, d), lambda i, j: (i, 0), pipeline_mode=pl.Buffered(1)),
        pl.BlockSpec((1, d), lambda i, j: (0, 0)),
        (pl.BlockSpec((1, tn, d), lambda i, j: (layer, j, 0)) if w_is_transposed
         else pl.BlockSpec((1, d, tn), lambda i, j: (layer, 0, j))),
    ]
    out_specs = [pl.BlockSpec((tm, tn), lambda i, j: (i, j))]
    out_shape = [jax.ShapeDtypeStruct((t, n), out_dtype)]
    args = [x, g.reshape(1, d), w]
    if has_gate:
        ng = w_gate.shape[0 if w_is_transposed else 1]
        in_specs.append(pl.BlockSpec(w_gate.shape, lambda i, j: (0, 0)))
        out_specs.append(pl.BlockSpec((tm, ng), lambda i, j: (i, 0)))
        out_shape.append(jax.ShapeDtypeStruct((t, ng), F32))
        args.append(w_gate)
    outs = pl.pallas_call(
        functools.partial(_norm_proj_kernel, row_chunk=64, has_gate=has_gate, w_is_transposed=w_is_transposed),
        grid=(t // tm, n // tn),
        in_specs=in_specs,
        out_specs=out_specs,
        out_shape=out_shape,
        scratch_shapes=[pltpu.VMEM((tm, d), BF16)],
        compiler_params=_params(("parallel", "arbitrary")),
        name="norm_proj",
    )(*args)
    return outs if has_gate else outs[0]


class _Stream:
    def __init__(self, **kw):
        self.__dict__.update(kw)


def _sublane_scan(x, op, identity, reverse):
    L = x.shape[0]
    row = lax.broadcasted_iota(jnp.int32, x.shape, 0)
    k = 1
    while k < L:
        if reverse:
            shifted, valid = pltpu.roll(x, L - k, 0), row < L - k
        else:
            shifted, valid = pltpu.roll(x, k, 0), row >= k
        x = op(x, jnp.where(valid, shifted, identity))
        k *= 2
    return x


def _mlstm_gates(ig, f_pre, m_prev, reverse):
    L = ig.shape[0]
    ln_scale = -0.5 * math.log(A_DK)
    last = slice(0, 1) if reverse else slice(L - 1, L)
    b = _sublane_scan(_log_sigmoid(f_pre), jnp.add, 0.0, reverse)
    u = ig - b
    cm = _sublane_scan(u, jnp.maximum, -jnp.inf, reverse)
    mx = jnp.maximum(cm, m_prev)
    mn = jnp.maximum(cm[last, :], m_prev)
    u2 = u * LOG2E
    g = _Stream(
        cv2=(ln_scale - mx) * LOG2E,
        rv2t=jnp.concatenate([u2[i * LANES:(i + 1) * LANES, :].T for i in range(L // LANES)], axis=1),
        w_inter=jnp.exp(m_prev - mx + ln_scale),
        floor=jnp.exp(-mx - b),
        w_s=jnp.exp(u - mn),
        decay=jnp.exp(m_prev - mn),
        m_new=b[last, :] + mn)
    return g


def _mlstm_chunk(streams):
    for s in streams:
        qk = lax.dot_general(s.q, s.k, _NT, preferred_element_type=F32)
        s.p = qk * jnp.exp2(s.cv2 + s.rv2 + s.negmask)
    for s in streams:
        num = (jnp.dot(s.p.astype(BF16), s.v, preferred_element_type=F32)
               + s.w_inter * jnp.dot(s.q, s.ct_ref[...].astype(BF16), preferred_element_type=F32))
        den = (jnp.sum(s.p, axis=-1, keepdims=True)
               + s.w_inter * jnp.sum(s.q.astype(F32) * s.n_ref[...], axis=-1, keepdims=True))
        s.h = num * (1.0 / jnp.maximum(jnp.abs(den), s.floor))
    for s in streams:
        kw = s.k.astype(F32) * s.w_s
        s.ct_ref[...] = s.decay * s.ct_ref[...] + jnp.dot(kw.T.astype(BF16), s.v, preferred_element_type=F32)
        s.n_ref[...] = s.decay * s.n_ref[...] + jnp.sum(kw, axis=0, keepdims=True)


def _mlstm_kernel(bi_ref, bf_ref, qf_ref, kf_ref, vf_ref, gif_ref, gff_ref, qb_ref, kb_ref, vb_ref, gib_ref,
                  gfb_ref, negf_ref, negb_ref, yf_ref, yb_ref, ctf, nf, mf, ctb, nb, mb, *, heads):
    @pl.when(pl.program_id(2) == 0)
    def _():
        for ref in (ctf, nf, mf, ctb, nb, mb):
            ref[...] = jnp.zeros_like(ref)

    gates_f = _mlstm_gates(gif_ref[0, 0] + bi_ref[0], gff_ref[0, 0] + bf_ref[0], mf[...], False)
    gates_b = _mlstm_gates(gib_ref[0, 0] + bi_ref[0], gfb_ref[0, 0] + bf_ref[0], mb[...], True)
    mf[...] = gates_f.m_new
    mb[...] = gates_b.m_new

    streams = []
    for hh in range(heads):
        ks = slice(hh * A_DK, (hh + 1) * A_DK)
        vs = slice(hh * A_DV, (hh + 1) * A_DV)
        for g, lane, q_ref, k_ref, v_ref, neg_ref, ct, n in (
                (gates_f, hh, qf_ref, kf_ref, vf_ref, negf_ref, ctf, nf),
                (gates_b, heads + hh, qb_ref, kb_ref, vb_ref, negb_ref, ctb, nb)):
            col = slice(lane, lane + 1)
            streams.append(_Stream(
                q=q_ref[:, ks].astype(BF16), k=k_ref[:, ks].astype(BF16), v=v_ref[:, vs].astype(BF16),
                cv2=g.cv2[:, col], rv2=g.rv2t[col, :], w_inter=g.w_inter[:, col], floor=g.floor[:, col],
                w_s=g.w_s[:, col], decay=g.decay[:, col], negmask=neg_ref[...],
                ct_ref=ct.at[hh], n_ref=n.at[hh]))
    _mlstm_chunk(streams)
    for hh in range(heads):
        vs = slice(hh * A_DV, (hh + 1) * A_DV)
        yf_ref[:, vs] = streams[2 * hh].h.astype(yf_ref.dtype)
        yb_ref[:, vs] = streams[2 * hh + 1].h.astype(yb_ref.dtype)


def _group_gate_lanes(x, batch, seq, heads):
    ng = A_HEADS // heads
    x = x[:, :4 * A_HEADS].reshape(batch, seq, 2, 2, ng, heads)
    x = jnp.transpose(x, (3, 0, 4, 1, 2, 5)).reshape(2, batch, ng, seq, 2 * heads)
    x = jnp.pad(x, ((0, 0),) * 4 + ((0, LANES - 2 * heads),))
    return x[0], x[1]


def mlstm_scan(p, gates, bias, batch, seq, heads=A_HEADS_PER_STEP):
    L = A_CHUNK
    nc = seq // L
    t = batch * seq
    ng = A_HEADS // heads
    g_in, g_forget = _group_gate_lanes(gates, batch, seq, heads)
    b_in, b_forget = _group_gate_lanes(bias.reshape(1, -1), 1, 1, heads)
    keep = np.tril(np.ones((L, L), np.float32))
    neg_f = jnp.asarray((1.0 - keep) * MASKED)
    neg_b = jnp.asarray((1.0 - keep.T) * MASKED)

    fwd = lambda b, h, c: b * nc + c
    bwd = lambda b, h, c: b * nc + (nc - 1 - c)

    def dir_specs(row):
        gate = pl.BlockSpec((1, 1, L, LANES), lambda b, h, c: (b, h, row(b, h, c) - b * nc, 0))
        return [
            pl.BlockSpec((L, heads * A_DK), lambda b, h, c: (row(b, h, c), h)),
            pl.BlockSpec((L, heads * A_DK), lambda b, h, c: (row(b, h, c), ng + h)),
            pl.BlockSpec((L, heads * A_DV), lambda b, h, c: (row(b, h, c), ng + h)),
            gate, gate,
        ]

    const = pl.BlockSpec((L, L), lambda b, h, c: (0, 0))
    bias_spec = pl.BlockSpec((1, 1, LANES), lambda b, h, c: (h, 0, 0))
    in_specs = [bias_spec, bias_spec] + dir_specs(fwd) + dir_specs(bwd) + [const, const]
    out_specs = [
        pl.BlockSpec((L, heads * A_DV), lambda b, h, c: (fwd(b, h, c), h)),
        pl.BlockSpec((L, heads * A_DV), lambda b, h, c: (bwd(b, h, c), h)),
    ]
    state = [pltpu.VMEM((heads, A_DK, A_DV), F32), pltpu.VMEM((heads, 1, A_DK), F32),
             pltpu.VMEM((1, LANES), F32)]
    return pl.pallas_call(
        functools.partial(_mlstm_kernel, heads=heads),
        grid=(batch, ng, nc),
        in_specs=in_specs,
        out_specs=out_specs,
        out_shape=[jax.ShapeDtypeStruct((t, A_HEADS * A_DV), BF16)] * 2,
        scratch_shapes=state + state,
        compiler_params=_params(("parallel", "parallel", "arbitrary")),
        name="mlstm_scan",
    )(b_in[0], b_forget[0], p, p, p, g_in, g_forget, p, p, p, g_in, g_forget, neg_f, neg_b)


def _mid_rows(b, c, r_off):
    L = b.shape[0]
    if c >= SUBLANES:
        b3 = b.reshape(L // c, c, LANES)
        return jnp.broadcast_to(b3[:, r_off:r_off + 1, :], b3.shape).reshape(L, LANES)
    b3 = b.reshape(L // SUBLANES, SUBLANES, LANES)
    lo = jnp.broadcast_to(b3[:, r_off:r_off + 1, :], b3.shape)
    hi = jnp.broadcast_to(b3[:, c + r_off:c + r_off + 1, :], b3.shape)
    sub = lax.broadcasted_iota(jnp.int32, b3.shape, 1)
    return jnp.where(sub < c, lo, hi).reshape(L, LANES)


def _midpoint_exponent(b, half, reverse):
    L = b.shape[0]
    pieces = []
    for lo in range(0, L, 2 * half):
        first, second = b[lo:lo + half, :], b[lo + half:lo + 2 * half, :]
        if reverse:
            mid = b[lo + half:lo + half + 1, :]
            pieces += [first - mid, mid - second]
        else:
            mid = b[lo + half - 1:lo + half, :]
            pieces += [mid - first, second - mid]
    return jnp.concatenate(pieces, axis=0)


def _split3(x):
    hi = x.astype(BF16)
    r1 = x - hi.astype(F32)
    mid = r1.astype(BF16)
    lo = (r1 - mid.astype(F32)).astype(BF16)
    return hi, mid, lo


def _hgrn_chunk(streams):
    L = streams[0].q.shape[0]
    row = lax.broadcasted_iota(jnp.int32, (L, LANES), 0)
    odd = (row & 1) == 1

    for s in streams:
        e = jnp.exp(-jnp.abs(s.f))
        r = 1.0 / (1.0 + e)
        er = e * r
        pos = s.f >= 0.0
        s.forget = s.lb + (1.0 - s.lb) * jnp.where(pos, r, er)
        s.kk = (1.0 - s.lb) * jnp.where(pos, er, r)
    for s in streams:
        s.b = jnp.dot(s.tri3, jnp.concatenate(_split3(jnp.log2(s.forget)), axis=0),
                      preferred_element_type=F32)
        s.b_tot = s.b[0:1, :] if s.reverse else s.b[L - 1:L, :]
    for s in streams:
        s.st = s.st_ref[...]
        s.k16 = s.kk.astype(BF16)
        s.o = lax.dot_general(s.q * jnp.exp2(s.b).astype(BF16), s.st.astype(BF16), _NT,
                              preferred_element_type=F32)
        s.attn = None

    for j in range(-1, B_LEVELS):
        half = 1 << max(j, 0)
        for s in streams:
            if j < 0:
                lhs, rhs = s.q, s.k16
            elif j == 0:
                w16 = jnp.where(odd != s.reverse, s.forget, 1.0).astype(BF16)
                lhs, rhs = s.q * w16, s.k16 * w16
            else:
                if half < SUBLANES:
                    b_mid = _mid_rows(s.b, 2 * half, half if s.reverse else half - 1)
                    w16 = jnp.exp2(-jnp.abs(s.b - b_mid)).astype(BF16)
                else:
                    w16 = jnp.exp2(_midpoint_exponent(s.b, half, s.reverse)).astype(BF16)
                lhs, rhs = s.q * w16, s.k16 * w16
            pj = (lax.dot_general(lhs, rhs, _NT, preferred_element_type=F32).astype(BF16)
                  * s.masks_ref[j % (B_LEVELS + 1)])
            s.attn = pj if s.attn is None else s.attn + pj
    for s in streams:
        s.o = s.o + jnp.dot(s.attn, s.v, preferred_element_type=F32)
    for s in streams:
        kt = s.k16 * jnp.exp2(s.b_tot - s.b).astype(BF16)
        s.st_ref[...] = jnp.exp2(s.b_tot) * s.st + jnp.dot(s.v.astype(F32).T.astype(BF16), kt,
                                                           preferred_element_type=F32)


def _hgrn_kernel(lbp_ref, qf_ref, vf_ref, ff_ref, qb_ref, vb_ref, fb_ref,
                 trif_ref, trib_ref, maskf_ref, maskb_ref, yf_ref, yb_ref, stf, stb, *, layer, heads):
    @pl.when(pl.program_id(2) == 0)
    def _():
        stf[...] = jnp.zeros_like(stf)
        stb[...] = jnp.zeros_like(stb)

    lbp = lbp_ref[...]
    ex = jnp.exp(lbp - jnp.max(lbp, axis=0, keepdims=True))
    sm = ex / jnp.sum(ex, axis=0, keepdims=True)
    lb = jnp.sum(sm[0:layer + 1, :], axis=0, keepdims=True) - sm[0:1, :]

    tri_f, tri_b = trif_ref[...], trib_ref[...]
    streams = []
    for hh in range(heads):
        ks = slice(hh * B_DK, (hh + 1) * B_DK)
        vs = slice(hh * B_DV, (hh + 1) * B_DV)
        streams.append(_Stream(q=qf_ref[:, ks], v=vf_ref[:, vs], f=ff_ref[:, ks].astype(F32), lb=lb[:, ks],
                               tri3=tri_f, masks_ref=maskf_ref, st_ref=stf.at[hh], reverse=False))
        streams.append(_Stream(q=qb_ref[:, ks], v=vb_ref[:, vs], f=fb_ref[:, ks].astype(F32), lb=lb[:, ks],
                               tri3=tri_b, masks_ref=maskb_ref, st_ref=stb.at[hh], reverse=True))
    _hgrn_chunk(streams)
    for hh in range(heads):
        vs = slice(hh * B_DV, (hh + 1) * B_DV)
        yf_ref[:, vs] = streams[2 * hh].o.astype(yf_ref.dtype)
        yb_ref[:, vs] = streams[2 * hh + 1].o.astype(yb_ref.dtype)


def _hgrn_masks(L):
    t = np.arange(L)[:, None]
    s = np.arange(L)[None, :]
    out = []
    for j in range(B_LEVELS):
        half = 1 << j
        c = 2 * half
        out.append(((t // c == s // c) & (t % c >= half) & (s % c < half)).astype(np.float32))
    out.append(np.eye(L, dtype=np.float32))
    return np.stack(out)


def hgrn_scan(p, lb_param, layer, batch, seq, heads=B_HEADS_PER_STEP):
    L = B_CHUNK
    nc = seq // L
    t = batch * seq
    ng = B_HEADS // heads
    tri_f = jnp.asarray(np.tile(np.tril(np.ones((L, L), np.float32)), (1, 3)), dtype=BF16)
    tri_b = jnp.asarray(np.tile(np.triu(np.ones((L, L), np.float32)), (1, 3)), dtype=BF16)
    masks_f = _hgrn_masks(L)
    masks_b = np.ascontiguousarray(np.transpose(masks_f, (0, 2, 1)))
    depth = lb_param.shape[0]

    fwd = lambda b, h, c: b * nc + c
    bwd = lambda b, h, c: b * nc + (nc - 1 - c)

    def dir_specs(row, fcol):
        return [
            pl.BlockSpec((L, heads * B_DK), lambda b, h, c: (row(b, h, c), h)),
            pl.BlockSpec((L, heads * B_DV), lambda b, h, c: (row(b, h, c), ng + h)),
            pl.BlockSpec((L, heads * B_DK), lambda b, h, c: (row(b, h, c), fcol * ng + h)),
        ]

    const2 = pl.BlockSpec((L, 3 * L), lambda b, h, c: (0, 0))
    const3 = pl.BlockSpec((B_LEVELS + 1, L, L), lambda b, h, c: (0, 0, 0))
    in_specs = ([pl.BlockSpec((depth, heads * B_DK), lambda b, h, c: (0, h))]
                + dir_specs(fwd, 3) + dir_specs(bwd, 4) + [const2, const2, const3, const3])
    out_specs = [
        pl.BlockSpec((L, heads * B_DV), lambda b, h, c: (fwd(b, h, c), h)),
        pl.BlockSpec((L, heads * B_DV), lambda b, h, c: (bwd(b, h, c), h)),
    ]
    return pl.pallas_call(
        functools.partial(_hgrn_kernel, layer=layer, heads=heads),
        grid=(batch, ng, nc),
        in_specs=in_specs,
        out_specs=out_specs,
        out_shape=[jax.ShapeDtypeStruct((t, B_HEADS * B_DV), BF16)] * 2,
        scratch_shapes=[pltpu.VMEM((heads, B_DV, B_DK), F32)] * 2,
        compiler_params=_params(("parallel", "parallel", "arbitrary")),
        name="hgrn_scan",
    )(lb_param, p, p, p, p, p, p, tri_f, tri_b, jnp.asarray(masks_f, dtype=BF16), jnp.asarray(masks_b, dtype=BF16))


def _mix_out_kernel(yf_ref, yb_ref, gate_ref, hg_ref, res_ref, w_ref, ng_ref, out_ref, hn_ref, w16, *, d_head, silu):
    d = res_ref.shape[1]

    @pl.when(pl.program_id(0) == 0)
    def _():
        for r0 in range(0, d, MXU_DIM):
            w16[r0:r0 + MXU_DIM, :] = w_ref[0, r0:r0 + MXU_DIM, :].astype(BF16)

    acc = res_ref[...]
    for c0 in range(0, d, MXU_DIM):
        parts = []
        for h0 in range(c0, c0 + MXU_DIM, d_head):
            sl = slice(h0, h0 + d_head)
            y = yf_ref[:, sl].astype(F32) + yb_ref[:, sl].astype(F32)
            ms = jnp.mean(y * y, axis=-1, keepdims=True)
            gate = gate_ref[:, sl].astype(F32)
            act = _sigmoid(gate)
            if silu:
                act = gate * act
            parts.append((y * lax.rsqrt(ms + EPS) * hg_ref[:, sl] * act).astype(BF16))
        z = parts[0] if len(parts) == 1 else jnp.concatenate(parts, axis=1)
        acc = acc + jnp.dot(z, w16[c0:c0 + MXU_DIM, :], preferred_element_type=F32)
    out_ref[...] = acc
    hn_ref[:, :d] = _rms_rows(acc, ng_ref[...]).astype(hn_ref.dtype)
    hn_ref[:, d:] = jnp.zeros((hn_ref.shape[0], hn_ref.shape[1] - d), hn_ref.dtype)


def mix_out(yf, yb, p, gate_col_block, head_g, res, w_out, layer, norm_g, d_head, silu, tm=256):
    t, d = yf.shape
    row = lambda i: (i, 0)
    const = lambda i: (0, 0)
    return pl.pallas_call(
        functools.partial(_mix_out_kernel, d_head=d_head, silu=silu),
        grid=(t // tm,),
        in_specs=[
            pl.BlockSpec((tm, d), row),
            pl.BlockSpec((tm, d), row),
            pl.BlockSpec((tm, d), lambda i: (i, gate_col_block)),
            pl.BlockSpec((1, d), const),
            pl.BlockSpec((tm, d), row),
            pl.BlockSpec((1, d, d), lambda i: (layer, 0, 0), pipeline_mode=pl.Buffered(1)),
            pl.BlockSpec((1, d), const),
        ],
        out_specs=[pl.BlockSpec((tm, d), row), pl.BlockSpec((tm, d + HN_PAD), row)],
        out_shape=[jax.ShapeDtypeStruct((t, d), F32), jax.ShapeDtypeStruct((t, d + HN_PAD), BF16)],
        scratch_shapes=[pltpu.VMEM((d, d), BF16)],
        compiler_params=_params(("arbitrary",)),
        name="mix_out",
    )(yf, yb, p, head_g.reshape(1, d), res, w_out, norm_g.reshape(1, d))


def _glu_up_kernel(hn_ref, wa0_ref, wa1_ref, wv0_ref, wv1_ref, cw_ref, cb_ref, g_ref, a_scr, v_scr, w16, *,
                   row_chunk):
    s = hn_ref.shape[0]
    d = w16.shape[0]
    tf = a_scr.shape[1]
    pad = SUBLANES
    zeros = jnp.zeros((pad, tf), F32)
    a_scr[0:pad, :] = zeros
    a_scr[pad + s:2 * pad + s, :] = zeros
    for i, w_ref in enumerate((wa0_ref, wa1_ref, wv0_ref, wv1_ref)):
        w16[:, i * LANES:(i + 1) * LANES] = w_ref[0].astype(BF16)

    def epilogue(first, last):
        n = last - first
        ext = a_scr[first:first + n + 2 * pad, :]
        prev = pltpu.roll(ext, 1, 0)[pad:pad + n, :]
        nxt = pltpu.roll(ext, n + 2 * pad - 1, 0)[pad:pad + n, :]
        c = (cb_ref[...] + prev * cw_ref[0:1, :] + ext[pad:pad + n, :] * cw_ref[1:2, :]
             + nxt * cw_ref[2:3, :])
        gelu = 0.5 * c * (1.0 + lax.erf(c * INV_SQRT2))
        g_ref[first:last, :] = (gelu * v_scr[first:last, :]).astype(BF16)

    trail = 2 * SUBLANES
    bounds = [0]
    while s - bounds[-1] > row_chunk:
        bounds.append(bounds[-1] + (s - bounds[-1]) // 2)
    bounds.append(s)
    done = 0
    for lo, hi in zip(bounds[:-1], bounds[1:]):
        if lo > 0:
            epilogue(done, lo - trail)
            done = lo - trail
        u = jnp.dot(hn_ref[lo:hi, :d], w16[...], preferred_element_type=F32)
        a_scr[pad + lo:pad + hi, :] = u[:, :tf]
        v_scr[lo:hi, :] = u[:, tf:]
    epilogue(done, s)


def glu_up(hn, w_up, layer, conv_w, conv_b, batch, seq):
    t = hn.shape[0]
    d = w_up.shape[1]
    nf = conv_w.shape[1]
    tf = FF_TILE
    nb = D_FF // LANES
    last = 2 * nb - 1
    wspec = lambda off: pl.BlockSpec((1, d, LANES), lambda b, j: (layer, 0, jnp.minimum(off + 2 * j, last)))
    return pl.pallas_call(
        functools.partial(_glu_up_kernel, row_chunk=256),
        grid=(batch, nf // tf),
        in_specs=[
            pl.BlockSpec((seq, hn.shape[1]), lambda b, j: (b, 0)),
            wspec(0), wspec(1), wspec(nb), wspec(nb + 1),
            pl.BlockSpec((3, tf), lambda b, j: (0, j)),
            pl.BlockSpec((1, tf), lambda b, j: (0, j)),
        ],
        out_specs=pl.BlockSpec((seq, tf), lambda b, j: (b, j)),
        out_shape=jax.ShapeDtypeStruct((t, D_FF), BF16),
        scratch_shapes=[pltpu.VMEM((seq + 2 * SUBLANES, tf), F32), pltpu.VMEM((seq, tf), F32),
                        pltpu.VMEM((d, 2 * tf), BF16)],
        compiler_params=_params(("parallel", "arbitrary")),
        name="glu_up",
    )(hn, w_up, w_up, w_up, w_up, conv_w, conv_b)


def _proj_res_kernel(x_ref, w_ref, res_ref, out_ref):
    out_ref[...] = res_ref[...] + jnp.dot(x_ref[...], w_ref[0], preferred_element_type=F32)


def proj_res(x, w, layer, res, tm=1024, tn=512):
    t, k = x.shape
    n = w.shape[2]
    return pl.pallas_call(
        _proj_res_kernel,
        grid=(t // tm, n // tn),
        in_specs=[
            pl.BlockSpec((tm, k), lambda i, j: (i, 0)),
            pl.BlockSpec((1, k, tn), lambda i, j: (layer, 0, j)),
            pl.BlockSpec((tm, tn), lambda i, j: (i, j)),
        ],
        out_specs=pl.BlockSpec((tm, tn), lambda i, j: (i, j)),
        out_shape=jax.ShapeDtypeStruct((t, n), F32),
        compiler_params=_params(("parallel", "arbitrary")),
        name="proj_res",
    )(x, w, res)


def _proj_res_norm_kernel(x_ref, w_ref, res_ref, g_ref, *out_refs, keep_sum):
    acc = res_ref[...] + jnp.dot(x_ref[...], w_ref[0], preferred_element_type=F32)
    if keep_sum:
        h_ref, hn_ref = out_refs
        h_ref[...] = acc
    else:
        (hn_ref,) = out_refs
    d = acc.shape[1]
    hn_ref[:, :d] = _rms_rows(acc, g_ref[...]).astype(hn_ref.dtype)
    if hn_ref.shape[1] > d:
        hn_ref[:, d:] = jnp.zeros((hn_ref.shape[0], hn_ref.shape[1] - d), hn_ref.dtype)


def proj_res_norm(x, w, layer, res, g, norm_dtype, norm_pad, keep_sum, tm=512):
    t, k = x.shape
    n = w.shape[2]
    row = lambda i: (i, 0)
    out_specs = [pl.BlockSpec((tm, n + norm_pad), row)]
    out_shape = [jax.ShapeDtypeStruct((t, n + norm_pad), norm_dtype)]
    if keep_sum:
        out_specs.insert(0, pl.BlockSpec((tm, n), row))
        out_shape.insert(0, jax.ShapeDtypeStruct((t, n), F32))
    outs = pl.pallas_call(
        functools.partial(_proj_res_norm_kernel, keep_sum=keep_sum),
        grid=(t // tm,),
        in_specs=[
            pl.BlockSpec((tm, k), row),
            pl.BlockSpec((1, k, n), lambda i: (layer, 0, 0), pipeline_mode=pl.Buffered(1)),
            pl.BlockSpec((tm, n), row),
            pl.BlockSpec((1, n), lambda i: (0, 0)),
        ],
        out_specs=out_specs,
        out_shape=out_shape,
        compiler_params=_params(("parallel",)),
        name="proj_res_norm",
    )(x, w, res, g.reshape(1, n))
    return outs if keep_sum else outs[0]


def _proj_kernel(x_ref, w_ref, o_ref):
    d = w_ref.shape[1]
    o_ref[...] = jnp.dot(x_ref[:, :d], w_ref[0].astype(BF16), preferred_element_type=F32).astype(o_ref.dtype)


def proj(x, w, layer, out_dtype, tm=2048, tn=512):
    t = x.shape[0]
    d, n = w.shape[1], w.shape[2]
    return pl.pallas_call(
        _proj_kernel,
        grid=(t // tm, n // tn),
        in_specs=[
            pl.BlockSpec((tm, x.shape[1]), lambda i, j: (i, 0)),
            pl.BlockSpec((1, d, tn), lambda i, j: (layer, 0, j)),
        ],
        out_specs=pl.BlockSpec((tm, tn), lambda i, j: (i, j)),
        out_shape=jax.ShapeDtypeStruct((t, n), out_dtype),
        compiler_params=_params(("parallel", "arbitrary")),
        name="proj",
    )(x, w)


def conv_glu_up(hn, w_up, conv_w, conv_b, layer, batch, seq):
    fpad = D_FF_PAD - D_FF
    cw = jnp.pad(conv_w[layer], ((0, 0), (0, fpad)))
    cb = jnp.pad(conv_b[layer], (0, fpad)).reshape(1, D_FF_PAD)
    return glu_up(hn, w_up, layer, cw, cb, batch, seq)


def kernel(x, norm_mix_g, norm_ffn_g, mlstm_w_in, mlstm_b_gate, mlstm_head_g, mlstm_w_out,
           hgrn_w_in, hgrn_lb, hgrn_head_g, hgrn_w_out, ffn_w_up, ffn_conv_w, ffn_conv_b,
           ffn_w_down, final_g):
    batch, seq, d = x.shape
    h = x.reshape(batch * seq, d)
    depth = norm_mix_g.shape[0]
    w_down16 = ffn_w_down.astype(BF16)
    hn_mix = None
    for layer in range(depth):
        j = layer // 2
        if layer % 2 == 0:
            n_gates = 4 * A_HEADS
            n_main = mlstm_w_in.shape[2] - n_gates
            w_in_t = jnp.swapaxes(mlstm_w_in, 1, 2)
            w_gate = w_in_t[j, -LANES:, :].astype(BF16)
            p, gates = norm_proj(h, norm_mix_g[layer], w_in_t, j, n_main, BF16, w_gate=w_gate,
                                 w_is_transposed=True)
            yf, yb = mlstm_scan(p, gates[:, LANES - n_gates:], mlstm_b_gate[j], batch, seq)
            h, hn = mix_out(yf, yb, p, 2, mlstm_head_g[j], h, mlstm_w_out, j, norm_ffn_g[layer], A_DV,
                            silu=False)
        else:
            if hn_mix is None:
                p = norm_proj(h, norm_mix_g[layer], hgrn_w_in, j, hgrn_w_in.shape[2], BF16)
            else:
                p = proj(hn_mix, hgrn_w_in, j, BF16)
            yf, yb = hgrn_scan(p, hgrn_lb, layer, batch, seq)
            h, hn = mix_out(yf, yb, p, 2, hgrn_head_g[j], h, hgrn_w_out, j, norm_ffn_g[layer], B_DV,
                            silu=True)
        g = conv_glu_up(hn, ffn_w_up, ffn_conv_w, ffn_conv_b, layer, batch, seq)
        if layer == depth - 1:
            return proj_res_norm(g, w_down16, layer, h, final_g, x.dtype, 0, keep_sum=False).reshape(batch, seq, d)
        if (layer + 1) % 2 == 1:
            h, hn_mix = proj_res_norm(g, w_down16, layer, h, norm_mix_g[layer + 1], BF16, HN_PAD, keep_sum=True)
        else:
            h, hn_mix = proj_res(g, w_down16, layer, h), None
```

```python
import functools
import math

import numpy as np
import jax
import jax.numpy as jnp
from jax import lax
from jax.experimental import pallas as pl
from jax.experimental.pallas import tpu as pltpu

F32 = jnp.float32
BF16 = jnp.bfloat16

EPS = 1e-6
D_MODEL = 2048
LANES = 128
SUBLANES = 8
MXU_DIM = 256
VMEM_LIMIT_BYTES = 56 * 1024 * 1024

A_HEADS, A_DK, A_DV = 8, 128, 256
A_CHUNK = 256
A_HEADS_PER_STEP = 4
B_HEADS, B_DK, B_DV = 16, 128, 128
B_CHUNK = 128
B_LEVELS = 7
B_HEADS_PER_STEP = 4
B_CHUNKS_PER_STEP = 2
D_FF = 5504
FF_TILE = 256
D_FF_PAD = 5632
HN_PAD = LANES
INV_SQRT2 = 0.7071067811865476
LOG2E = 1.4426950408889634
MASKED = -1e30

_NT = (((1,), (1,)), ((), ()))


def _params(sem):
    return pltpu.CompilerParams(dimension_semantics=sem, vmem_limit_bytes=VMEM_LIMIT_BYTES)


def _rms_rows(x, g):
    ms = jnp.mean(x * x, axis=-1, keepdims=True)
    return x * lax.rsqrt(ms + EPS) * g


def _sigmoid(x):
    return 1.0 / (1.0 + jnp.exp(-x))


def _log_sigmoid(x):
    return jnp.minimum(x, 0.0) - jnp.log1p(jnp.exp(-jnp.abs(x)))


def _norm_proj_kernel(x_ref, g_ref, w_ref, *rest, row_chunk, has_gate, w_is_transposed):
    if has_gate:
        wg_ref, o_ref, og_ref, hn_ref = rest
    else:
        o_ref, hn_ref = rest
    contract = _NT if w_is_transposed else (((1,), (0,)), ((), ()))

    @pl.when(pl.program_id(1) == 0)
    def _():
        def body(r, carry):
            rows = pl.ds(pl.multiple_of(r * row_chunk, row_chunk), row_chunk)
            hn_ref[rows, :] = _rms_rows(x_ref[rows, :], g_ref[...]).astype(BF16)
            return carry

        lax.fori_loop(0, x_ref.shape[0] // row_chunk, body, 0)
        if has_gate:
            og_ref[...] = lax.dot_general(hn_ref[...], wg_ref[...], contract, preferred_element_type=F32)

    o_ref[...] = lax.dot_general(hn_ref[...], w_ref[0].astype(BF16), contract,
                                 preferred_element_type=F32).astype(o_ref.dtype)


def norm_proj(x, g, w, layer, n, out_dtype, w_gate=None, w_is_transposed=False, tm=2048, tn=512):
    t, d = x.shape
    has_gate = w_gate is not None
    in_specs = [
        pl.BlockSpec((tm, d), lambda i, j: (i, 0), pipeline_mode=pl.Buffered(1)),
        pl.BlockSpec((1, d), lambda i, j: (0, 0)),
        (pl.BlockSpec((1, tn, d), lambda i, j: (layer, j, 0)) if w_is_transposed
         else pl.BlockSpec((1, d, tn), lambda i, j: (layer, 0, j))),
    ]
    out_specs = [pl.BlockSpec((tm, tn), lambda i, j: (i, j))]
    out_shape = [jax.ShapeDtypeStruct((t, n), out_dtype)]
    args = [x, g.reshape(1, d), w]
    if has_gate:
        ng = w_gate.shape[0 if w_is_transposed else 1]
        in_specs.append(pl.BlockSpec(w_gate.shape, lambda i, j: (0, 0)))
        out_specs.append(pl.BlockSpec((tm, ng), lambda i, j: (i, 0)))
        out_shape.append(jax.ShapeDtypeStruct((t, ng), F32))
        args.append(w_gate)
    outs = pl.pallas_call(
        functools.partial(_norm_proj_kernel, row_chunk=64, has_gate=has_gate, w_is_transposed=w_is_transposed),
        grid=(t // tm, n // tn),
        in_specs=in_specs,
        out_specs=out_specs,
        out_shape=out_shape,
        scratch_shapes=[pltpu.VMEM((tm, d), BF16)],
        compiler_params=_params(("parallel", "arbitrary")),
        name="norm_proj",
    )(*args)
    return outs if has_gate else outs[0]


class _Stream:
    def __init__(self, **kw):
        self.__dict__.update(kw)


def _sublane_scan(x, op, identity, reverse):
    L = x.shape[0]
    row = lax.broadcasted_iota(jnp.int32, x.shape, 0)
    k = 1
    while k < L:
        if reverse:
            shifted, valid = pltpu.roll(x, L - k, 0), row < L - k
        else:
            shifted, valid = pltpu.roll(x, k, 0), row >= k
        x = op(x, jnp.where(valid, shifted, identity))
        k *= 2
    return x


def _mlstm_gates(ig, f_pre, m_prev, reverse):
    L = ig.shape[0]
    ln_scale = -0.5 * math.log(A_DK)
    last = slice(0, 1) if reverse else slice(L - 1, L)
    b = _sublane_scan(_log_sigmoid(f_pre), jnp.add, 0.0, reverse)
    u = ig - b
    cm = _sublane_scan(u, jnp.maximum, -jnp.inf, reverse)
    mx = jnp.maximum(cm, m_prev)
    mn = jnp.maximum(cm[last, :], m_prev)
    u2 = u * LOG2E
    g = _Stream(
        cv2=(ln_scale - mx) * LOG2E,
        rv2t=jnp.concatenate([u2[i * LANES:(i + 1) * LANES, :].T for i in range(L // LANES)], axis=1),
        w_inter=jnp.exp(m_prev - mx + ln_scale),
        floor=jnp.exp(-mx - b),
        w_s=jnp.exp(u - mn),
        decay=jnp.exp(m_prev - mn),
        m_new=b[last, :] + mn)
    return g


def _mlstm_chunk(streams):
    for s in streams:
        qk = lax.dot_general(s.q, s.k, _NT, preferred_element_type=F32)
        s.p = qk * jnp.exp2(s.cv2 + s.rv2 + s.negmask)
    for s in streams:
        num = (jnp.dot(s.p.astype(BF16), s.v, preferred_element_type=F32)
               + s.w_inter * jnp.dot(s.q, s.ct_ref[...].astype(BF16), preferred_element_type=F32))
        den = (jnp.sum(s.p, axis=-1, keepdims=True)
               + s.w_inter * jnp.sum(s.q.astype(F32) * s.n_ref[...], axis=-1, keepdims=True))
        s.h = num * (1.0 / jnp.maximum(jnp.abs(den), s.floor))
    for s in streams:
        kw = s.k.astype(F32) * s.w_s
        s.ct_ref[...] = s.decay * s.ct_ref[...] + jnp.dot(kw.T.astype(BF16), s.v, preferred_element_type=F32)
        s.n_ref[...] = s.decay * s.n_ref[...] + jnp.sum(kw, axis=0, keepdims=True)


def _mlstm_kernel(bi_ref, bf_ref, qf_ref, kf_ref, vf_ref, gif_ref, gff_ref, qb_ref, kb_ref, vb_ref, gib_ref,
                  gfb_ref, negf_ref, negb_ref, yf_ref, yb_ref, ctf, nf, mf, ctb, nb, mb, *, heads):
    @pl.when(pl.program_id(2) == 0)
    def _():
        for ref in (ctf, nf, mf, ctb, nb, mb):
            ref[...] = jnp.zeros_like(ref)

    gates_f = _mlstm_gates(gif_ref[0, 0] + bi_ref[0], gff_ref[0, 0] + bf_ref[0], mf[...], False)
    gates_b = _mlstm_gates(gib_ref[0, 0] + bi_ref[0], gfb_ref[0, 0] + bf_ref[0], mb[...], True)
    mf[...] = gates_f.m_new
    mb[...] = gates_b.m_new

    streams = []
    for hh in range(heads):
        ks = slice(hh * A_DK, (hh + 1) * A_DK)
        vs = slice(hh * A_DV, (hh + 1) * A_DV)
        for g, lane, q_ref, k_ref, v_ref, neg_ref, ct, n in (
                (gates_f, hh, qf_ref, kf_ref, vf_ref, negf_ref, ctf, nf),
                (gates_b, heads + hh, qb_ref, kb_ref, vb_ref, negb_ref, ctb, nb)):
            col = slice(lane, lane + 1)
            streams.append(_Stream(
                q=q_ref[:, ks].astype(BF16), k=k_ref[:, ks].astype(BF16), v=v_ref[:, vs].astype(BF16),
                cv2=g.cv2[:, col], rv2=g.rv2t[col, :], w_inter=g.w_inter[:, col], floor=g.floor[:, col],
                w_s=g.w_s[:, col], decay=g.decay[:, col], negmask=neg_ref[...],
                ct_ref=ct.at[hh], n_ref=n.at[hh]))
    _mlstm_chunk(streams)
    for hh in range(heads):
        vs = slice(hh * A_DV, (hh + 1) * A_DV)
        yf_ref[:, vs] = streams[2 * hh].h.astype(yf_ref.dtype)
        yb_ref[:, vs] = streams[2 * hh + 1].h.astype(yb_ref.dtype)


def _group_gate_lanes(x, batch, seq, heads):
    ng = A_HEADS // heads
    x = x[:, :4 * A_HEADS].reshape(batch, seq, 2, 2, ng, heads)
    x = jnp.transpose(x, (3, 0, 4, 1, 2, 5)).reshape(2, batch, ng, seq, 2 * heads)
    x = jnp.pad(x, ((0, 0),) * 4 + ((0, LANES - 2 * heads),))
    return x[0], x[1]


def mlstm_scan(p, gates, bias, batch, seq, heads=A_HEADS_PER_STEP):
    L = A_CHUNK
    nc = seq // L
    t = batch * seq
    ng = A_HEADS // heads
    g_in, g_forget = _group_gate_lanes(gates, batch, seq, heads)
    b_in, b_forget = _group_gate_lanes(bias.reshape(1, -1), 1, 1, heads)
    keep = np.tril(np.ones((L, L), np.float32))
    neg_f = jnp.asarray((1.0 - keep) * MASKED)
    neg_b = jnp.asarray((1.0 - keep.T) * MASKED)

    fwd = lambda b, h, c: b * nc + c
    bwd = lambda b, h, c: b * nc + (nc - 1 - c)

    def dir_specs(row):
        gate = pl.BlockSpec((1, 1, L, LANES), lambda b, h, c: (b, h, row(b, h, c) - b * nc, 0))
        return [
            pl.BlockSpec((L, heads * A_DK), lambda b, h, c: (row(b, h, c), h)),
            pl.BlockSpec((L, heads * A_DK), lambda b, h, c: (row(b, h, c), ng + h)),
            pl.BlockSpec((L, heads * A_DV), lambda b, h, c: (row(b, h, c), ng + h)),
            gate, gate,
        ]

    const = pl.BlockSpec((L, L), lambda b, h, c: (0, 0))
    bias_spec = pl.BlockSpec((1, 1, LANES), lambda b, h, c: (h, 0, 0))
    in_specs = [bias_spec, bias_spec] + dir_specs(fwd) + dir_specs(bwd) + [const, const]
    out_specs = [
        pl.BlockSpec((L, heads * A_DV), lambda b, h, c: (fwd(b, h, c), h)),
        pl.BlockSpec((L, heads * A_DV), lambda b, h, c: (bwd(b, h, c), h)),
    ]
    state = [pltpu.VMEM((heads, A_DK, A_DV), F32), pltpu.VMEM((heads, 1, A_DK), F32),
             pltpu.VMEM((1, LANES), F32)]
    return pl.pallas_call(
        functools.partial(_mlstm_kernel, heads=heads),
        grid=(batch, ng, nc),
        in_specs=in_specs,
        out_specs=out_specs,
        out_shape=[jax.ShapeDtypeStruct((t, A_HEADS * A_DV), BF16)] * 2,
        scratch_shapes=state + state,
        compiler_params=_params(("parallel", "parallel", "arbitrary")),
        name="mlstm_scan",
    )(b_in[0], b_forget[0], p, p, p, g_in, g_forget, p, p, p, g_in, g_forget, neg_f, neg_b)


def _mid_rows(b, c, r_off):
    L = b.shape[0]
    if c >= SUBLANES:
        b3 = b.reshape(L // c, c, LANES)
        return jnp.broadcast_to(b3[:, r_off:r_off + 1, :], b3.shape).reshape(L, LANES)
    b3 = b.reshape(L // SUBLANES, SUBLANES, LANES)
    lo = jnp.broadcast_to(b3[:, r_off:r_off + 1, :], b3.shape)
    hi = jnp.broadcast_to(b3[:, c + r_off:c + r_off + 1, :], b3.shape)
    sub = lax.broadcasted_iota(jnp.int32, b3.shape, 1)
    return jnp.where(sub < c, lo, hi).reshape(L, LANES)


def _midpoint_exponent(b, half, reverse):
    L = b.shape[0]
    pieces = []
    for lo in range(0, L, 2 * half):
        first, second = b[lo:lo + half, :], b[lo + half:lo + 2 * half, :]
        if reverse:
            mid = b[lo + half:lo + half + 1, :]
            pieces += [first - mid, mid - second]
        else:
            mid = b[lo + half - 1:lo + half, :]
            pieces += [mid - first, second - mid]
    return jnp.concatenate(pieces, axis=0)


def _split3(x):
    hi = x.astype(BF16)
    r1 = x - hi.astype(F32)
    mid = r1.astype(BF16)
    lo = (r1 - mid.astype(F32)).astype(BF16)
    return hi, mid, lo


def _hgrn_chunk(streams):
    L = streams[0].q.shape[0]
    row = lax.broadcasted_iota(jnp.int32, (L, LANES), 0)
    odd = (row & 1) == 1

    for s in streams:
        e = jnp.exp(-jnp.abs(s.f))
        r = 1.0 / (1.0 + e)
        er = e * r
        pos = s.f >= 0.0
        s.forget = s.lb + (1.0 - s.lb) * jnp.where(pos, r, er)
        s.kk = (1.0 - s.lb) * jnp.where(pos, er, r)
    for s in streams:
        s.b = jnp.dot(s.tri3, jnp.concatenate(_split3(jnp.log2(s.forget)), axis=0),
                      preferred_element_type=F32)
        s.b_tot = s.b[0:1, :] if s.reverse else s.b[L - 1:L, :]
    for s in streams:
        s.st = s.st_ref[...]
        s.k16 = s.kk.astype(BF16)
        s.o = lax.dot_general(s.q * jnp.exp2(s.b).astype(BF16), s.st.astype(BF16), _NT,
                              preferred_element_type=F32)
        s.attn = None

    for j in range(-1, B_LEVELS):
        half = 1 << max(j, 0)
        for s in streams:
            if j < 0:
                lhs, rhs = s.q, s.k16
            elif j == 0:
                w16 = jnp.where(odd != s.reverse, s.forget, 1.0).astype(BF16)
                lhs, rhs = s.q * w16, s.k16 * w16
            else:
                if half < SUBLANES:
                    b_mid = _mid_rows(s.b, 2 * half, half if s.reverse else half - 1)
                    w16 = jnp.exp2(-jnp.abs(s.b - b_mid)).astype(BF16)
                else:
                    w16 = jnp.exp2(_midpoint_exponent(s.b, half, s.reverse)).astype(BF16)
                lhs, rhs = s.q * w16, s.k16 * w16
            pj = (lax.dot_general(lhs, rhs, _NT, preferred_element_type=F32).astype(BF16)
                  * s.masks_ref[j % (B_LEVELS + 1)])
            s.attn = pj if s.attn is None else s.attn + pj
    for s in streams:
        s.o = s.o + jnp.dot(s.attn, s.v, preferred_element_type=F32)
    for s in streams:
        kt = s.k16 * jnp.exp2(s.b_tot - s.b).astype(BF16)
        s.st_ref[...] = jnp.exp2(s.b_tot) * s.st + jnp.dot(s.v.astype(F32).T.astype(BF16), kt,
                                                           preferred_element_type=F32)


def _hgrn_kernel(lbp_ref, qf_ref, vf_ref, ff_ref, qb_ref, vb_ref, fb_ref,
                 trif_ref, trib_ref, maskf_ref, maskb_ref, yf_ref, yb_ref, stf, stb, *, layer, heads):
    @pl.when(pl.program_id(2) == 0)
    def _():
        stf[...] = jnp.zeros_like(stf)
        stb[...] = jnp.zeros_like(stb)

    lbp = lbp_ref[...]
    ex = jnp.exp(lbp - jnp.max(lbp, axis=0, keepdims=True))
    sm = ex / jnp.sum(ex, axis=0, keepdims=True)
    lb = jnp.sum(sm[0:layer + 1, :], axis=0, keepdims=True) - sm[0:1, :]

    L = B_CHUNK
    n_sub = qf_ref.shape[0] // L

    def sub_chunk(i, carry):
        rows_f = pl.ds(pl.multiple_of(i * L, L), L)
        rows_b = pl.ds(pl.multiple_of((n_sub - 1 - i) * L, L), L)
        tri_f, tri_b = trif_ref[...], trib_ref[...]
        streams = []
        for hh in range(heads):
            ks = slice(hh * B_DK, (hh + 1) * B_DK)
            vs = slice(hh * B_DV, (hh + 1) * B_DV)
            streams.append(_Stream(q=qf_ref[rows_f, ks], v=vf_ref[rows_f, vs], f=ff_ref[rows_f, ks].astype(F32),
                                   lb=lb[:, ks], tri3=tri_f, masks_ref=maskf_ref, st_ref=stf.at[hh],
                                   reverse=False))
            streams.append(_Stream(q=qb_ref[rows_b, ks], v=vb_ref[rows_b, vs], f=fb_ref[rows_b, ks].astype(F32),
                                   lb=lb[:, ks], tri3=tri_b, masks_ref=maskb_ref, st_ref=stb.at[hh],
                                   reverse=True))
        _hgrn_chunk(streams)
        for hh in range(heads):
            vs = slice(hh * B_DV, (hh + 1) * B_DV)
            yf_ref[rows_f, vs] = streams[2 * hh].o.astype(yf_ref.dtype)
            yb_ref[rows_b, vs] = streams[2 * hh + 1].o.astype(yb_ref.dtype)
        return carry

    lax.fori_loop(0, n_sub, sub_chunk, 0)


def _hgrn_masks(L):
    t = np.arange(L)[:, None]
    s = np.arange(L)[None, :]
    out = []
    for j in range(B_LEVELS):
        half = 1 << j
        c = 2 * half
        out.append(((t // c == s // c) & (t % c >= half) & (s % c < half)).astype(np.float32))
    out.append(np.eye(L, dtype=np.float32))
    return np.stack(out)


def hgrn_scan(p, lb_param, layer, batch, seq, heads=B_HEADS_PER_STEP):
    L = B_CHUNK
    rows = L * B_CHUNKS_PER_STEP
    nc = seq // rows
    t = batch * seq
    ng = B_HEADS // heads
    tri_f = jnp.asarray(np.tile(np.tril(np.ones((L, L), np.float32)), (1, 3)), dtype=BF16)
    tri_b = jnp.asarray(np.tile(np.triu(np.ones((L, L), np.float32)), (1, 3)), dtype=BF16)
    masks_f = _hgrn_masks(L)
    masks_b = np.ascontiguousarray(np.transpose(masks_f, (0, 2, 1)))
    depth = lb_param.shape[0]

    fwd = lambda b, h, c: b * nc + c
    bwd = lambda b, h, c: b * nc + (nc - 1 - c)

    def dir_specs(row, fcol):
        return [
            pl.BlockSpec((rows, heads * B_DK), lambda b, h, c: (row(b, h, c), h)),
            pl.BlockSpec((rows, heads * B_DV), lambda b, h, c: (row(b, h, c), ng + h)),
            pl.BlockSpec((rows, heads * B_DK), lambda b, h, c: (row(b, h, c), fcol * ng + h)),
        ]

    const2 = pl.BlockSpec((L, 3 * L), lambda b, h, c: (0, 0))
    const3 = pl.BlockSpec((B_LEVELS + 1, L, L), lambda b, h, c: (0, 0, 0))
    in_specs = ([pl.BlockSpec((depth, heads * B_DK), lambda b, h, c: (0, h))]
                + dir_specs(fwd, 3) + dir_specs(bwd, 4) + [const2, const2, const3, const3])
    out_specs = [
        pl.BlockSpec((rows, heads * B_DV), lambda b, h, c: (fwd(b, h, c), h)),
        pl.BlockSpec((rows, heads * B_DV), lambda b, h, c: (bwd(b, h, c), h)),
    ]
    return pl.pallas_call(
        functools.partial(_hgrn_kernel, layer=layer, heads=heads),
        grid=(batch, ng, nc),
        in_specs=in_specs,
        out_specs=out_specs,
        out_shape=[jax.ShapeDtypeStruct((t, B_HEADS * B_DV), BF16)] * 2,
        scratch_shapes=[pltpu.VMEM((heads, B_DV, B_DK), F32)] * 2,
        compiler_params=_params(("parallel", "parallel", "arbitrary")),
        name="hgrn_scan",
    )(lb_param, p, p, p, p, p, p, tri_f, tri_b, jnp.asarray(masks_f, dtype=BF16), jnp.asarray(masks_b, dtype=BF16))


def _mix_out_kernel(yf_ref, yb_ref, gate_ref, hg_ref, res_ref, w_ref, ng_ref, out_ref, hn_ref, w16, *, d_head, silu):
    d = res_ref.shape[1]

    @pl.when(pl.program_id(0) == 0)
    def _():
        for r0 in range(0, d, MXU_DIM):
            w16[r0:r0 + MXU_DIM, :] = w_ref[0, r0:r0 + MXU_DIM, :].astype(BF16)

    acc = res_ref[...]
    for c0 in range(0, d, MXU_DIM):
        parts = []
        for h0 in range(c0, c0 + MXU_DIM, d_head):
            sl = slice(h0, h0 + d_head)
            y = yf_ref[:, sl].astype(F32) + yb_ref[:, sl].astype(F32)
            ms = jnp.mean(y * y, axis=-1, keepdims=True)
            gate = gate_ref[:, sl].astype(F32)
            act = _sigmoid(gate)
            if silu:
                act = gate * act
            parts.append((y * lax.rsqrt(ms + EPS) * hg_ref[:, sl] * act).astype(BF16))
        z = parts[0] if len(parts) == 1 else jnp.concatenate(parts, axis=1)
        acc = acc + jnp.dot(z, w16[c0:c0 + MXU_DIM, :], preferred_element_type=F32)
    out_ref[...] = acc
    hn_ref[:, :d] = _rms_rows(acc, ng_ref[...]).astype(hn_ref.dtype)
    hn_ref[:, d:] = jnp.zeros((hn_ref.shape[0], hn_ref.shape[1] - d), hn_ref.dtype)


def mix_out(yf, yb, p, gate_col_block, head_g, res, w_out, layer, norm_g, d_head, silu, tm=256):
    t, d = yf.shape
    row = lambda i: (i, 0)
    const = lambda i: (0, 0)
    return pl.pallas_call(
        functools.partial(_mix_out_kernel, d_head=d_head, silu=silu),
        grid=(t // tm,),
        in_specs=[
            pl.BlockSpec((tm, d), row),
            pl.BlockSpec((tm, d), row),
            pl.BlockSpec((tm, d), lambda i: (i, gate_col_block)),
            pl.BlockSpec((1, d), const),
            pl.BlockSpec((tm, d), row),
            pl.BlockSpec((1, d, d), lambda i: (layer, 0, 0), pipeline_mode=pl.Buffered(1)),
            pl.BlockSpec((1, d), const),
        ],
        out_specs=[pl.BlockSpec((tm, d), row), pl.BlockSpec((tm, d + HN_PAD), row)],
        out_shape=[jax.ShapeDtypeStruct((t, d), F32), jax.ShapeDtypeStruct((t, d + HN_PAD), BF16)],
        scratch_shapes=[pltpu.VMEM((d, d), BF16)],
        compiler_params=_params(("arbitrary",)),
        name="mix_out",
    )(yf, yb, p, head_g.reshape(1, d), res, w_out, norm_g.reshape(1, d))


def _glu_up_kernel(hn_ref, wa0_ref, wa1_ref, wv0_ref, wv1_ref, cw_ref, cb_ref, g_ref, a_scr, v_scr, w16, *,
                   row_chunk):
    s = hn_ref.shape[0]
    d = w16.shape[0]
    tf = a_scr.shape[1]
    pad = SUBLANES
    zeros = jnp.zeros((pad, tf), F32)
    a_scr[0:pad, :] = zeros
    a_scr[pad + s:2 * pad + s, :] = zeros
    for i, w_ref in enumerate((wa0_ref, wa1_ref, wv0_ref, wv1_ref)):
        w16[:, i * LANES:(i + 1) * LANES] = w_ref[0].astype(BF16)

    def epilogue(first, last):
        n = last - first
        ext = a_scr[first:first + n + 2 * pad, :]
        prev = pltpu.roll(ext, 1, 0)[pad:pad + n, :]
        nxt = pltpu.roll(ext, n + 2 * pad - 1, 0)[pad:pad + n, :]
        c = (cb_ref[...] + prev * cw_ref[0:1, :] + ext[pad:pad + n, :] * cw_ref[1:2, :]
             + nxt * cw_ref[2:3, :])
        gelu = 0.5 * c * (1.0 + lax.erf(c * INV_SQRT2))
        g_ref[first:last, :] = (gelu * v_scr[first:last, :]).astype(BF16)

    trail = 2 * SUBLANES
    bounds = [0]
    while s - bounds[-1] > row_chunk:
        bounds.append(bounds[-1] + (s - bounds[-1]) // 2)
    bounds.append(s)
    done = 0
    for lo, hi in zip(bounds[:-1], bounds[1:]):
        if lo > 0:
            epilogue(done, lo - trail)
            done = lo - trail
        u = jnp.dot(hn_ref[lo:hi, :d], w16[...], preferred_element_type=F32)
        a_scr[pad + lo:pad + hi, :] = u[:, :tf]
        v_scr[lo:hi, :] = u[:, tf:]
    epilogue(done, s)


def glu_up(hn, w_up, layer, conv_w, conv_b, batch, seq):
    t = hn.shape[0]
    d = w_up.shape[1]
    nf = conv_w.shape[1]
    tf = FF_TILE
    nb = D_FF // LANES
    last = 2 * nb - 1
    wspec = lambda off: pl.BlockSpec((1, d, LANES), lambda b, j: (layer, 0, jnp.minimum(off + 2 * j, last)))
    return pl.pallas_call(
        functools.partial(_glu_up_kernel, row_chunk=256),
        grid=(batch, nf // tf),
        in_specs=[
            pl.BlockSpec((seq, hn.shape[1]), lambda b, j: (b, 0)),
            wspec(0), wspec(1), wspec(nb), wspec(nb + 1),
            pl.BlockSpec((3, tf), lambda b, j: (0, j)),
            pl.BlockSpec((1, tf), lambda b, j: (0, j)),
        ],
        out_specs=pl.BlockSpec((seq, tf), lambda b, j: (b, j)),
        out_shape=jax.ShapeDtypeStruct((t, D_FF), BF16),
        scratch_shapes=[pltpu.VMEM((seq + 2 * SUBLANES, tf), F32), pltpu.VMEM((seq, tf), F32),
                        pltpu.VMEM((d, 2 * tf), BF16)],
        compiler_params=_params(("parallel", "arbitrary")),
        name="glu_up",
    )(hn, w_up, w_up, w_up, w_up, conv_w, conv_b)


def _proj_res_kernel(x_ref, w_ref, res_ref, out_ref):
    out_ref[...] = res_ref[...] + jnp.dot(x_ref[...], w_ref[0], preferred_element_type=F32)


def proj_res(x, w, layer, res, tm=1024, tn=512):
    t, k = x.shape
    n = w.shape[2]
    return pl.pallas_call(
        _proj_res_kernel,
        grid=(t // tm, n // tn),
        in_specs=[
            pl.BlockSpec((tm, k), lambda i, j: (i, 0)),
            pl.BlockSpec((1, k, tn), lambda i, j: (layer, 0, j)),
            pl.BlockSpec((tm, tn), lambda i, j: (i, j)),
        ],
        out_specs=pl.BlockSpec((tm, tn), lambda i, j: (i, j)),
        out_shape=jax.ShapeDtypeStruct((t, n), F32),
        compiler_params=_params(("parallel", "arbitrary")),
        name="proj_res",
    )(x, w, res)


def _proj_res_norm_kernel(x_ref, w_ref, res_ref, g_ref, *out_refs, keep_sum):
    acc = res_ref[...] + jnp.dot(x_ref[...], w_ref[0], preferred_element_type=F32)
    if keep_sum:
        h_ref, hn_ref = out_refs
        h_ref[...] = acc
    else:
        (hn_ref,) = out_refs
    d = acc.shape[1]
    hn_ref[:, :d] = _rms_rows(acc, g_ref[...]).astype(hn_ref.dtype)
    if hn_ref.shape[1] > d:
        hn_ref[:, d:] = jnp.zeros((hn_ref.shape[0], hn_ref.shape[1] - d), hn_ref.dtype)


def proj_res_norm(x, w, layer, res, g, norm_dtype, norm_pad, keep_sum, tm=512):
    t, k = x.shape
    n = w.shape[2]
    row = lambda i: (i, 0)
    out_specs = [pl.BlockSpec((tm, n + norm_pad), row)]
    out_shape = [jax.ShapeDtypeStruct((t, n + norm_pad), norm_dtype)]
    if keep_sum:
        out_specs.insert(0, pl.BlockSpec((tm, n), row))
        out_shape.insert(0, jax.ShapeDtypeStruct((t, n), F32))
    outs = pl.pallas_call(
        functools.partial(_proj_res_norm_kernel, keep_sum=keep_sum),
        grid=(t // tm,),
        in_specs=[
            pl.BlockSpec((tm, k), row),
            pl.BlockSpec((1, k, n), lambda i: (layer, 0, 0), pipeline_mode=pl.Buffered(1)),
            pl.BlockSpec((tm, n), row),
            pl.BlockSpec((1, n), lambda i: (0, 0)),
        ],
        out_specs=out_specs,
        out_shape=out_shape,
        compiler_params=_params(("parallel",)),
        name="proj_res_norm",
    )(x, w, res, g.reshape(1, n))
    return outs if keep_sum else outs[0]


def _proj_kernel(x_ref, w_ref, o_ref):
    d = w_ref.shape[1]
    o_ref[...] = jnp.dot(x_ref[:, :d], w_ref[0].astype(BF16), preferred_element_type=F32).astype(o_ref.dtype)


def proj(x, w, layer, out_dtype, tm=2048, tn=512):
    t = x.shape[0]
    d, n = w.shape[1], w.shape[2]
    return pl.pallas_call(
        _proj_kernel,
        grid=(t // tm, n // tn),
        in_specs=[
            pl.BlockSpec((tm, x.shape[1]), lambda i, j: (i, 0)),
            pl.BlockSpec((1, d, tn), lambda i, j: (layer, 0, j)),
        ],
        out_specs=pl.BlockSpec((tm, tn), lambda i, j: (i, j)),
        out_shape=jax.ShapeDtypeStruct((t, n), out_dtype),
        compiler_params=_params(("parallel", "arbitrary")),
        name="proj",
    )(x, w)


def conv_glu_up(hn, w_up, conv_w, conv_b, layer, batch, seq):
    fpad = D_FF_PAD - D_FF
    cw = jnp.pad(conv_w[layer], ((0, 0), (0, fpad)))
    cb = jnp.pad(conv_b[layer], (0, fpad)).reshape(1, D_FF_PAD)
    return glu_up(hn, w_up, layer, cw, cb, batch, seq)


def kernel(x, norm_mix_g, norm_ffn_g, mlstm_w_in, mlstm_b_gate, mlstm_head_g, mlstm_w_out,
           hgrn_w_in, hgrn_lb, hgrn_head_g, hgrn_w_out, ffn_w_up, ffn_conv_w, ffn_conv_b,
           ffn_w_down, final_g):
    batch, seq, d = x.shape
    h = x.reshape(batch * seq, d)
    depth = norm_mix_g.shape[0]
    w_down16 = ffn_w_down.astype(BF16)
    hn_mix = None
    for layer in range(depth):
        j = layer // 2
        if layer % 2 == 0:
            n_gates = 4 * A_HEADS
            n_main = mlstm_w_in.shape[2] - n_gates
            w_in_t = jnp.swapaxes(mlstm_w_in, 1, 2)
            w_gate = w_in_t[j, -LANES:, :].astype(BF16)
            p, gates = norm_proj(h, norm_mix_g[layer], w_in_t, j, n_main, BF16, w_gate=w_gate,
                                 w_is_transposed=True)
            yf, yb = mlstm_scan(p, gates[:, LANES - n_gates:], mlstm_b_gate[j], batch, seq)
            h, hn = mix_out(yf, yb, p, 2, mlstm_head_g[j], h, mlstm_w_out, j, norm_ffn_g[layer], A_DV,
                            silu=False)
        else:
            if hn_mix is None:
                p = norm_proj(h, norm_mix_g[layer], hgrn_w_in, j, hgrn_w_in.shape[2], BF16)
            else:
                p = proj(hn_mix, hgrn_w_in, j, BF16)
            yf, yb = hgrn_scan(p, hgrn_lb, layer, batch, seq)
            h, hn = mix_out(yf, yb, p, 2, hgrn_head_g[j], h, hgrn_w_out, j, norm_ffn_g[layer], B_DV,
                            silu=True)
        g = conv_glu_up(hn, ffn_w_up, ffn_conv_w, ffn_conv_b, layer, batch, seq)
        if layer == depth - 1:
            return proj_res_norm(g, w_down16, layer, h, final_g, x.dtype, 0, keep_sum=False).reshape(batch, seq, d)
        if (layer + 1) % 2 == 1:
            h, hn_mix = proj_res_norm(g, w_down16, layer, h, norm_mix_g[layer + 1], BF16, HN_PAD, keep_sum=True)
        else:
            h, hn_mix = proj_res(g, w_down16, layer, h), None
```
